```python
import math
import jax, jax.numpy as jnp
from jax import lax
import numpy as np

D_MODEL = 1024
BATCH = 32
SEQ = 256
DEPTH = 4
DEC_BATCH = 4
DEC_SEQ = 2048
PAST_LEN = 512

GRID_W = 64
NA_HEADS = 8
NA_DH = 64
NA_ROWS = 8
NA_COLS = 16
NA_QCOL_BLOCK = 16
NA_KCOL_BLOCK = 32
DIFF_HEADS = 4
DIFF_DH = 64
RET_HEADS = 8
RET_DH = 64
RET_CHUNK = 128
LRU_WIDTH = 512
LRU_BLOCKS = 8
LRU_CONV = 4
LRU_C = 8.0
N_BRANCH = 4
BRANCH_W = 512
D_FF = 4 * D_MODEL
ROPE_BASE = 10000.0
QUERY_BLOCK = 128
EPS = 1e-6
NEG_INF = -1e30

NA_W = NA_HEADS * NA_DH
DIFF_QK_W = 2 * DIFF_HEADS * DIFF_DH
DIFF_V_W = DIFF_HEADS * 2 * DIFF_DH
RET_W = RET_HEADS * RET_DH
IN_WIDTHS = (NA_W, NA_W, NA_W, DIFF_QK_W, DIFF_QK_W, DIFF_V_W, RET_W, RET_W, RET_W, RET_W, LRU_WIDTH, LRU_WIDTH, N_BRANCH * D_MODEL)
IN_WIDTH = sum(IN_WIDTHS)

kernel_name = "hybrid_flow_backbone_ctx_prefix_step"


def rmsnorm(x, g):
    xf = x.astype(jnp.float32)
    y = xf * lax.rsqrt(jnp.mean(jnp.square(xf), axis=-1, keepdims=True) + EPS)
    return (y * g.astype(jnp.float32)).astype(x.dtype)


def modulation(cond, ada_w, ada_b):
    m = jax.nn.silu(cond) @ ada_w + ada_b
    return [part[:, None, :] for part in jnp.split(m, 6, axis=-1)]


def split_cols(z):
    outs = []
    start = 0
    for w in IN_WIDTHS:
        outs.append(z[..., start:start + w])
        start += w
    return outs


def split_heads(z, n_heads):
    b, t, _ = z.shape
    return z.reshape(b, t, n_heads, -1).transpose(0, 2, 1, 3)


def merge_heads(z):
    b, h, t, d = z.shape
    return z.transpose(0, 2, 1, 3).reshape(b, t, h * d)


def diff_qk_heads(z):
    b, t, _ = z.shape
    return z.reshape(b, t, 2, DIFF_HEADS, DIFF_DH).transpose(0, 2, 3, 1, 4)


def axial_angles(n, dim):
    t = jnp.arange(n)
    quarter = dim // 4
    inv = ROPE_BASE ** (-jnp.arange(quarter, dtype=jnp.float32) / quarter)
    ang_r = (t // GRID_W).astype(jnp.float32)[:, None] * inv
    ang_c = (t % GRID_W).astype(jnp.float32)[:, None] * inv
    return ang_r, ang_c


def rotate(x, ang):
    x1, x2 = jnp.split(x, 2, axis=-1)
    cos = jnp.cos(ang).astype(x.dtype)
    sin = jnp.sin(ang).astype(x.dtype)
    return jnp.concatenate([x1 * cos - x2 * sin, x2 * cos + x1 * sin], axis=-1)


def axial_rope(x, ang_r, ang_c):
    xr, xc = jnp.split(x, 2, axis=-1)
    return jnp.concatenate([rotate(xr, ang_r), rotate(xc, ang_c)], axis=-1)


def dense_attention(q, k, v):
    b, h, nq, d = q.shape
    nb = nq // QUERY_BLOCK
    scale = d ** -0.5
    qb = q.reshape(b, h, nb, QUERY_BLOCK, d).transpose(2, 0, 1, 3, 4)

    def one_block(qi):
        s = jnp.einsum('bhqd,bhkd->bhqk', qi, k).astype(jnp.float32) * scale
        p = jax.nn.softmax(s, axis=-1).astype(v.dtype)
        return jnp.einsum('bhqk,bhkd->bhqd', p, v)

    o = lax.map(one_block, qb)
    return o.transpose(1, 2, 0, 3, 4).reshape(b, h, nq, d)


def diff_attention(q, k, v, lam):
    b, _, h, nq, d = q.shape
    nb = nq // QUERY_BLOCK
    scale = d ** -0.5
    qb = q.reshape(b, 2, h, nb, QUERY_BLOCK, d).transpose(3, 0, 1, 2, 4, 5)

    def one_block(qi):
        s = jnp.einsum('bchqd,bchkd->bchqk', qi, k).astype(jnp.float32) * scale
        p = jax.nn.softmax(s, axis=-1)
        a = (p[:, 0] - lam * p[:, 1]).astype(v.dtype)
        return jnp.einsum('bhqk,bhkd->bhqd', a, v)

    o = lax.map(one_block, qb)
    return o.transpose(1, 2, 0, 3, 4).reshape(b, h, nq, v.shape[-1])


def neighbourhood_attention(q, k, v, k_ctx, v_ctx, rpb):
    b, h, n, d = q.shape
    rows = n // GRID_W
    kr = min(NA_ROWS, rows)
    ncb = GRID_W // NA_QCOL_BLOCK
    nloc = kr * NA_KCOL_BLOCK
    scale = d ** -0.5
    qcol = np.arange(GRID_W).reshape(ncb, NA_QCOL_BLOCK)
    kc0 = np.clip(np.arange(ncb) * NA_QCOL_BLOCK - NA_COLS // 2, 0, GRID_W - NA_KCOL_BLOCK)
    kcol = kc0[:, None] + np.arange(NA_KCOL_BLOCK)[None, :]
    cstart = np.clip(qcol - NA_COLS // 2, 0, GRID_W - NA_COLS)
    col_ok = jnp.asarray((kcol[:, None, :] >= cstart[:, :, None]) & (kcol[:, None, :] < cstart[:, :, None] + NA_COLS))
    dc_idx = np.clip(kcol[:, None, :] - qcol[:, :, None] + NA_COLS - 1, 0, 2 * NA_COLS - 2)
    bias_cols = rpb[:, :, dc_idx]
    kg = k.reshape(b, h, rows, GRID_W, d)
    vg = v.reshape(b, h, rows, GRID_W, d)
    qg = q.reshape(b, h, rows, ncb, NA_QCOL_BLOCK, d).transpose(2, 0, 1, 3, 4, 5)

    def one_row(args):
        r, qr = args
        rs = jnp.clip(r - kr // 2, 0, rows - kr)
        k_win = lax.dynamic_slice_in_dim(kg, rs, kr, axis=2)[:, :, :, kcol].transpose(0, 1, 3, 2, 4, 5)
        v_win = lax.dynamic_slice_in_dim(vg, rs, kr, axis=2)[:, :, :, kcol].transpose(0, 1, 3, 2, 4, 5)
        dr = rs + jnp.arange(kr) - r + NA_ROWS - 1
        bias = bias_cols[:, dr].transpose(0, 2, 3, 1, 4).astype(jnp.float32)
        s_loc = jnp.einsum('bhjqd,bhjikd->bhjqik', qr, k_win).astype(jnp.float32) * scale + bias
        s_loc = jnp.where(col_ok[:, :, None, :], s_loc, NEG_INF)
        s_ctx = jnp.einsum('bhjqd,bhpd->bhjqp', qr, k_ctx).astype(jnp.float32) * scale
        s = jnp.concatenate([s_loc.reshape(b, h, ncb, NA_QCOL_BLOCK, nloc), s_ctx], axis=-1)
        p = jax.nn.softmax(s, axis=-1).astype(v.dtype)
        o_loc = jnp.einsum('bhjqn,bhjnd->bhjqd', p[..., :nloc], v_win.reshape(b, h, ncb, nloc, d))
        o_ctx = jnp.einsum('bhjqp,bhpd->bhjqd', p[..., nloc:], v_ctx)
        return o_loc + o_ctx

    o = lax.map(one_row, (jnp.arange(rows), qg))
    return o.transpose(1, 2, 0, 3, 4, 5).reshape(b, h, n, d)


def retention_chunkwise(q, k, v, log_g, s0, include_diag):
    b, h, t, _ = q.shape
    dv = v.shape[-1]
    nc = t // RET_CHUNK
    f32 = jnp.float32
    pos = jnp.arange(RET_CHUNK, dtype=f32)
    rel = pos[:, None] - pos[None, :]
    mask = rel >= 0 if include_diag else rel > 0
    lg = log_g.astype(f32)
    dmat = jnp.where(mask, jnp.exp(jnp.where(mask, rel, 0.0) * lg[:, None, None]), 0.0)
    xi = jnp.exp((pos + 1.0) * lg[:, None])[..., None]
    zeta = jnp.exp((RET_CHUNK - 1.0 - pos) * lg[:, None])[..., None]
    g_chunk = jnp.exp(RET_CHUNK * lg)[:, None, None]

    def chunks(z):
        return z.astype(f32).reshape(b, h, nc, RET_CHUNK, -1).transpose(2, 0, 1, 3, 4)

    def step(s, qkv):
        qc, kc, vc = qkv
        inner = jnp.einsum('bhnd,bhmd->bhnm', qc, kc) * dmat
        o = jnp.einsum('bhnm,bhme->bhne', inner, vc) + jnp.einsum('bhnd,bhde->bhne', qc, s) * xi
        s = g_chunk * s + jnp.einsum('bhmd,bhme->bhde', kc * zeta, vc)
        return s, o

    s_fin, o = lax.scan(step, s0.astype(f32), (chunks(q), chunks(k), chunks(v)))
    return o.transpose(1, 2, 0, 3, 4).reshape(b, h, t, dv), s_fin


def retention_bidir(q, k, v, lg_f, lg_b, s_f0, s_b0):
    o_f, s_f = retention_chunkwise(q, k, v, lg_f, s_f0, True)
    o_b, s_b = retention_chunkwise(q[:, :, ::-1], k[:, :, ::-1], v[:, :, ::-1], lg_b, s_b0, False)
    return o_f + o_b[:, :, ::-1], s_f, s_b


def centred_dwconv(x, w, bias):
    ch = x.shape[-1]
    y = lax.conv_general_dilated(x, w[:, None, :].astype(x.dtype), window_strides=(1,),
                                 padding=[((LRU_CONV - 1) // 2, LRU_CONV // 2)],
                                 dimension_numbers=('NWC', 'WIO', 'NWC'), feature_group_count=ch)
    return y + bias


def rglru_gates(x, wa, ba, wx, bx, lam):
    b, t, w = x.shape
    xf = x.astype(jnp.float32)
    xb = xf.reshape(b, t, LRU_BLOCKS, -1)
    r = jax.nn.sigmoid(jnp.einsum('btnd,nde->btne', xb, wa.astype(jnp.float32)).reshape(b, t, w) + ba.astype(jnp.float32))
    i = jax.nn.sigmoid(jnp.einsum('btnd,nde->btne', xb, wx.astype(jnp.float32)).reshape(b, t, w) + bx.astype(jnp.float32))
    log_a = -LRU_C * r * jax.nn.softplus(-lam.astype(jnp.float32))
    a = jnp.exp(log_a)
    u = jnp.sqrt(-jnp.expm1(2.0 * log_a)) * (i * xf)
    return a, u


def linear_scan(a, u, h0):
    u = u.at[:, 0].add(a[:, 0] * h0.astype(jnp.float32))

    def combine(left, right):
        a_l, u_l = left
        a_r, u_r = right
        return a_l * a_r, a_r * u_l + u_r

    _, hs = lax.associative_scan(combine, (a, u), axis=1)
    return hs


def rglru_bidir(x, p_fwd, p_bwd, h_f0, h_b0):
    a, u = rglru_gates(x, *p_fwd)
    h_f = linear_scan(a, u, h_f0)
    a, u = rglru_gates(x[:, ::-1], *p_bwd)
    h_b = linear_scan(a, u, h_b0)
    return h_f + h_b[:, ::-1], h_f[:, -1], h_b[:, -1]


def merge_branches(ys, gate_logits, w_branch, w_out):
    y = jnp.stack(ys, axis=-2)
    proj = jnp.einsum('btkw,kwd->btkd', y, w_branch)
    g = jax.nn.sigmoid(gate_logits.reshape(gate_logits.shape[:-1] + (N_BRANCH, D_MODEL)))
    return jnp.sum(g * proj, axis=-2) @ w_out


def mixer(h, P, l, ctx):
    b, t, _ = h.shape
    (na_q, na_k, na_v, df_q, df_k, df_v, rt_q, rt_k, rt_v, rt_g, lr_x, lr_g, gate_logits) = split_cols(h @ P['w_in'])
    qa, ka, va = split_heads(na_q, NA_HEADS), split_heads(na_k, NA_HEADS), split_heads(na_v, NA_HEADS)
    qb, kb, vb = diff_qk_heads(df_q), diff_qk_heads(df_k), split_heads(df_v, DIFF_HEADS)
    lam_init = 0.8 - 0.6 * math.exp(-0.3 * l)
    lam = (jnp.exp(jnp.sum(P['diff_lq1'].astype(jnp.float32) * P['diff_lk1'].astype(jnp.float32)))
           - jnp.exp(jnp.sum(P['diff_lq2'].astype(jnp.float32) * P['diff_lk2'].astype(jnp.float32))) + lam_init)
    qc = split_heads(rt_q, RET_HEADS)
    kc = split_heads(rt_k, RET_HEADS) * (RET_DH ** -0.5)
    vc = split_heads(rt_v, RET_HEADS)
    lg_f = jnp.log1p(-jnp.exp(P['ret_theta_fwd'].astype(jnp.float32)))
    lg_b = jnp.log1p(-jnp.exp(P['ret_theta_bwd'].astype(jnp.float32)))
    xd = centred_dwconv(lr_x, P['lru_conv_w'], P['lru_conv_b'])
    if ctx is None:
        ya_h = dense_attention(qa, ka, va)
        yb_h = diff_attention(qb, kb, vb, lam)
        s_rf0 = jnp.zeros((b, RET_HEADS, RET_DH, RET_DH), jnp.float32)
        s_rb0 = jnp.zeros((b, RET_HEADS, RET_DH, RET_DH), jnp.float32)
        h_lf0 = jnp.zeros((b, LRU_WIDTH), jnp.float32)
        h_lb0 = jnp.zeros((b, LRU_WIDTH), jnp.float32)
    else:
        (c_na_k, c_na_v, c_df_k, c_df_v, s_rf0, s_rb0, h_lf0, h_lb0) = ctx
        ya_h = neighbourhood_attention(qa, ka, va, c_na_k, c_na_v, P['na_rpb'])
        ang_r, ang_c = axial_angles(t, DIFF_DH)
        qb_rot = axial_rope(qb, ang_r, ang_c)
        kb_rot = axial_rope(kb, ang_r, ang_c)
        yb_h = diff_attention(qb_rot, jnp.concatenate([kb_rot, c_df_k], axis=3),
                              jnp.concatenate([vb, c_df_v], axis=2), lam)
    oc, s_rf, s_rb = retention_bidir(qc, kc, vc, lg_f, lg_b, s_rf0, s_rb0)
    hd, h_lf, h_lb = rglru_bidir(xd, P['lru_fwd'], P['lru_bwd'], h_lf0, h_lb0)
    ya = merge_heads(ya_h)
    yb = merge_heads(rmsnorm(yb_h, P['diff_norm']) * (1.0 - lam_init))
    yc = merge_heads(rmsnorm(oc, P['ret_norm'].reshape(RET_HEADS, 1, RET_DH)).astype(h.dtype)) * jax.nn.silu(rt_g)
    yd = hd.astype(h.dtype) * jax.nn.gelu(lr_g)
    out = merge_branches((ya, yb, yc, yd), gate_logits, P['w_branch'], P['w_out'])
    return out, (ka, va, kb, vb, s_rf, s_rb, h_lf, h_lb)


def trunk_layer(x, cond, P, l, ctx):
    sh1, sc1, g1, sh2, sc2, g2 = modulation(cond, P['ada_w'], P['ada_b'])
    h = rmsnorm(x, P['norm_mix_pre']) * (1.0 + sc1) + sh1
    y, ctx_out = mixer(h, P, l, ctx)
    x = x + g1 * rmsnorm(y, P['norm_mix_post'])
    h = rmsnorm(x, P['norm_ffn_pre']) * (1.0 + sc2) + sh2
    y = jnp.square(jax.nn.relu(h @ P['mlp_w1'])) @ P['mlp_w2']
    x = x + g2 * rmsnorm(y, P['norm_ffn_post'])
    return x, ctx_out


def setup_inputs(seed: int = 0) -> dict:
    key = jax.random.key(seed)
    keys = iter(jax.random.split(key, 64))
    D = D_MODEL

    def nrm(shape, scale=1.0):
        return jax.random.normal(next(keys), shape, jnp.float32) * scale

    def gain(shape):
        return 1.0 + nrm(shape, 0.05)

    def lru_lambda():
        u = jax.random.uniform(next(keys), (DEPTH, LRU_WIDTH), jnp.float32, 0.9, 0.999)
        s = u ** (1.0 / LRU_C)
        return jnp.log(s) - jnp.log1p(-s)

    theta0 = jnp.linspace(math.log(1.0 / 32.0), math.log(1.0 / 512.0), RET_HEADS, dtype=jnp.float32)
    bw = LRU_WIDTH // LRU_BLOCKS
    return {
        'x_prompt': nrm((BATCH, SEQ, D)),
        'x_sample': nrm((DEC_BATCH, DEC_SEQ, D)),
        'cache_na_k': nrm((DEC_BATCH, DEPTH, NA_HEADS, PAST_LEN, NA_DH)),
        'cache_na_v': nrm((DEC_BATCH, DEPTH, NA_HEADS, PAST_LEN, NA_DH)),
        'cache_diff_k': nrm((DEC_BATCH, DEPTH, 2, DIFF_HEADS, PAST_LEN, DIFF_DH)),
        'cache_diff_v': nrm((DEC_BATCH, DEPTH, DIFF_HEADS, PAST_LEN, 2 * DIFF_DH)),
        'state_ret_fwd': nrm((DEC_BATCH, DEPTH, RET_HEADS, RET_DH, RET_DH), 0.5),
        'state_ret_bwd': nrm((DEC_BATCH, DEPTH, RET_HEADS, RET_DH, RET_DH), 0.5),
        'state_lru_fwd': nrm((DEC_BATCH, DEPTH, LRU_WIDTH), 0.5),
        'state_lru_bwd': nrm((DEC_BATCH, DEPTH, LRU_WIDTH), 0.5),
        'c': nrm((DEC_BATCH, D)),
        'c_ctx': nrm((D,)),
        'ada_w': nrm((DEPTH, D, 6 * D), 0.5 * D ** -0.5),
        'ada_b': nrm((DEPTH, 6 * D), 0.02),
        'norm_mix_pre': gain((DEPTH, D)),
        'norm_mix_post': gain((DEPTH, D)),
        'norm_ffn_pre': gain((DEPTH, D)),
        'norm_ffn_post': gain((DEPTH, D)),
        'w_in': nrm((DEPTH, D, IN_WIDTH), D ** -0.5),
        'na_rpb': nrm((DEPTH, NA_HEADS, 2 * NA_ROWS - 1, 2 * NA_COLS - 1), 0.5),
        'diff_lq1': nrm((DEPTH, DIFF_DH), 0.1),
        'diff_lk1': nrm((DEPTH, DIFF_DH), 0.1),
        'diff_lq2': nrm((DEPTH, DIFF_DH), 0.1),
        'diff_lk2': nrm((DEPTH, DIFF_DH), 0.1),
        'diff_norm': gain((DEPTH, 2 * DIFF_DH)),
        'ret_theta_fwd': theta0 + nrm((DEPTH, RET_HEADS), 0.05),
        'ret_theta_bwd': theta0 + nrm((DEPTH, RET_HEADS), 0.05),
        'ret_norm': gain((DEPTH, RET_W)),
        'lru_conv_w': nrm((DEPTH, LRU_CONV, LRU_WIDTH), LRU_CONV ** -0.5),
        'lru_conv_b': nrm((DEPTH, LRU_WIDTH), 0.02),
        'lru_wa_fwd': nrm((DEPTH, LRU_BLOCKS, bw, bw), bw ** -0.5),
        'lru_ba_fwd': nrm((DEPTH, LRU_WIDTH), 0.02),
        'lru_wx_fwd': nrm((DEPTH, LRU_BLOCKS, bw, bw), bw ** -0.5),
        'lru_bx_fwd': nrm((DEPTH, LRU_WIDTH), 0.02),
        'lru_lam_fwd': lru_lambda(),
        'lru_wa_bwd': nrm((DEPTH, LRU_BLOCKS, bw, bw), bw ** -0.5),
        'lru_ba_bwd': nrm((DEPTH, LRU_WIDTH), 0.02),
        'lru_wx_bwd': nrm((DEPTH, LRU_BLOCKS, bw, bw), bw ** -0.5),
        'lru_bx_bwd': nrm((DEPTH, LRU_WIDTH), 0.02),
        'lru_lam_bwd': lru_lambda(),
        'w_branch': nrm((DEPTH, N_BRANCH, BRANCH_W, D), BRANCH_W ** -0.5),
        'w_out': nrm((DEPTH, D, D), D ** -0.5),
        'mlp_w1': nrm((DEPTH, D, D_FF), D ** -0.5),
        'mlp_w2': nrm((DEPTH, D_FF, D), D_FF ** -0.5),
    }


def reference(x_prompt, x_sample, cache_na_k, cache_na_v, cache_diff_k, cache_diff_v,
              state_ret_fwd, state_ret_bwd, state_lru_fwd, state_lru_bwd, c, c_ctx,
              ada_w, ada_b, norm_mix_pre, norm_mix_post, norm_ffn_pre, norm_ffn_post, w_in, na_rpb,
              diff_lq1, diff_lk1, diff_lq2, diff_lk2, diff_norm, ret_theta_fwd, ret_theta_bwd, ret_norm,
              lru_conv_w, lru_conv_b, lru_wa_fwd, lru_ba_fwd, lru_wx_fwd, lru_bx_fwd, lru_lam_fwd,
              lru_wa_bwd, lru_ba_bwd, lru_wx_bwd, lru_bx_bwd, lru_lam_bwd,
              w_branch, w_out, mlp_w1, mlp_w2):
    y_p = x_prompt
    y_s = x_sample
    ctx_outs = []
    for l in range(DEPTH):
        P = {
            'ada_w': ada_w[l], 'ada_b': ada_b[l],
            'norm_mix_pre': norm_mix_pre[l], 'norm_mix_post': norm_mix_post[l],
            'norm_ffn_pre': norm_ffn_pre[l], 'norm_ffn_post': norm_ffn_post[l],
            'w_in': w_in[l], 'na_rpb': na_rpb[l],
            'diff_lq1': diff_lq1[l], 'diff_lk1': diff_lk1[l], 'diff_lq2': diff_lq2[l], 'diff_lk2': diff_lk2[l],
            'diff_norm': diff_norm[l],
            'ret_theta_fwd': ret_theta_fwd[l], 'ret_theta_bwd': ret_theta_bwd[l], 'ret_norm': ret_norm[l],
            'lru_conv_w': lru_conv_w[l], 'lru_conv_b': lru_conv_b[l],
            'lru_fwd': (lru_wa_fwd[l], lru_ba_fwd[l], lru_wx_fwd[l], lru_bx_fwd[l], lru_lam_fwd[l]),
            'lru_bwd': (lru_wa_bwd[l], lru_ba_bwd[l], lru_wx_bwd[l], lru_bx_bwd[l], lru_lam_bwd[l]),
            'w_branch': w_branch[l], 'w_out': w_out[l],
            'mlp_w1': mlp_w1[l], 'mlp_w2': mlp_w2[l],
        }
        y_p, ctx_l = trunk_layer(y_p, c_ctx[None, :], P, l, None)
        ctx_outs.append(ctx_l)
        cached = (cache_na_k[:, l], cache_na_v[:, l], cache_diff_k[:, l], cache_diff_v[:, l],
                  state_ret_fwd[:, l], state_ret_bwd[:, l], state_lru_fwd[:, l], state_lru_bwd[:, l])
        y_s, _ = trunk_layer(y_s, c, P, l, cached)
    new_na_k = jnp.stack([o[0] for o in ctx_outs], axis=1)
    new_na_v = jnp.stack([o[1] for o in ctx_outs], axis=1)
    new_diff_k = jnp.stack([o[2] for o in ctx_outs], axis=1)
    new_diff_v = jnp.stack([o[3] for o in ctx_outs], axis=1)
    new_ret_fwd = jnp.stack([o[4] for o in ctx_outs], axis=1)
    new_ret_bwd = jnp.stack([o[5] for o in ctx_outs], axis=1)
    new_lru_fwd = jnp.stack([o[6] for o in ctx_outs], axis=1)
    new_lru_bwd = jnp.stack([o[7] for o in ctx_outs], axis=1)
    return (y_p, y_s, new_na_k, new_na_v, new_diff_k, new_diff_v, new_ret_fwd, new_ret_bwd, new_lru_fwd, new_lru_bwd)
```

```python
import functools
import math

import numpy as np
import jax
import jax.numpy as jnp
from jax import lax
from jax.experimental import pallas as pl
from jax.experimental.pallas import tpu as pltpu

F32 = jnp.float32
BF16 = jnp.bfloat16

D_MODEL = 1024
BATCH = 32
SEQ = 256
DEPTH = 4
DEC_BATCH = 4
DEC_SEQ = 2048
PAST_LEN = 512
GRID_W = 64
NA_HEADS = 8
NA_DH = 64
NA_ROWS = 8
NA_COLS = 16
DIFF_HEADS = 4
DIFF_DH = 64
RET_HEADS = 8
RET_DH = 64
RET_CHUNK = 128
LRU_WIDTH = 512
LRU_BLOCKS = 8
LRU_CONV = 4
LRU_C = 8.0
N_BRANCH = 4
BRANCH_W = 512
D_FF = 4 * D_MODEL
ROPE_BASE = 10000.0
EPS = 1e-6
NEG_INF = -1e30

COL_NA_Q, COL_NA_K, COL_NA_V = 0, 512, 1024
COL_DF_Q, COL_DF_K, COL_DF_V = 1536, 2048, 2560
COL_RT_Q, COL_RT_K, COL_RT_V, COL_RT_G = 3072, 3584, 4096, 4608
COL_LR_X, COL_LR_G = 5120, 5632
COL_GATE = 6144
IN_WIDTH = COL_GATE + N_BRANCH * D_MODEL

N_COND = 1 + DEC_BATCH
COND_PAD = 8
LANES = 128
SUBLANES = 8
VMEM_LIMIT = 56 * 1024 * 1024


def _params(*sem):
    return pltpu.CompilerParams(dimension_semantics=sem, vmem_limit_bytes=VMEM_LIMIT)


def _dot(a, b):
    return jnp.dot(a.astype(BF16), b.astype(BF16), preferred_element_type=F32)


def _dot_nt(a, b):
    return lax.dot_general(a.astype(BF16), b.astype(BF16), (((1,), (1,)), ((), ())),
                           preferred_element_type=F32)


def _dot_tn(a, b):
    return lax.dot_general(a.astype(BF16), b.astype(BF16), (((0,), (0,)), ((), ())),
                           preferred_element_type=F32)


def _rms(x, g):
    return x * lax.rsqrt(jnp.mean(x * x, axis=-1, keepdims=True) + EPS) * g


def _sigmoid(x):
    return 1.0 / (1.0 + jnp.exp(-x))


def _softmax_parts(parts):
    m = functools.reduce(jnp.maximum, [jnp.max(s, axis=-1, keepdims=True) for s in parts])
    es = [jnp.exp(s - m) for s in parts]
    inv = 1.0 / functools.reduce(lambda a, b: a + b, [jnp.sum(e, axis=-1, keepdims=True) for e in es])
    return [e * inv for e in es]


def _mod_kernel(ct_ref, w_ref, b_ref, o_ref):
    ct = ct_ref[...]
    s = ct * _sigmoid(ct)
    w = w_ref[0]
    rows = [jnp.sum(s[:, r:r + 1] * w, axis=0, keepdims=True) + b_ref[0] for r in range(N_COND)]
    rows.append(jnp.zeros((COND_PAD - N_COND, w.shape[1]), F32))
    o_ref[0] = jnp.concatenate(rows, axis=0)


def _modulation(cond_t, ada_w, ada_b):
    tn = 768
    n = 6 * D_MODEL
    return pl.pallas_call(
        _mod_kernel,
        grid=(DEPTH, n // tn),
        in_specs=[pl.BlockSpec((D_MODEL, COND_PAD), lambda l, j: (0, 0)),
                  pl.BlockSpec((1, D_MODEL, tn), lambda l, j: (l, 0, j)),
                  pl.BlockSpec((1, 1, tn), lambda l, j: (l, 0, j))],
        out_specs=pl.BlockSpec((1, COND_PAD, tn), lambda l, j: (l, 0, j)),
        out_shape=jax.ShapeDtypeStruct((DEPTH, COND_PAD, n), F32),
        compiler_params=_params("parallel", "parallel"),
    )(cond_t, ada_w, ada_b.reshape(DEPTH, 1, n))


def _inproj_kernel(x_ref, m_ref, g_ref, w_ref, z_ref, h_scr):
    @pl.when(pl.program_id(1) == 0)
    def _():
        h = _rms(x_ref[...], g_ref[...]) * (1.0 + m_ref[0, 1:2, :]) + m_ref[0, 0:1, :]
        h_scr[...] = h.astype(BF16)

    z_ref[...] = jnp.dot(h_scr[...], w_ref[...], preferred_element_type=F32)


def _inproj(x, mod, gain, w_bf, mod_row):
    n_tok = x.shape[0]
    tm, tn = 1024, 1024
    return pl.pallas_call(
        _inproj_kernel,
        grid=(n_tok // tm, IN_WIDTH // tn),
        in_specs=[pl.BlockSpec((tm, D_MODEL), lambda i, j: (i, 0)),
                  pl.BlockSpec((1, 6, D_MODEL), lambda i, j: (mod_row(i, tm), 0, 0)),
                  pl.BlockSpec((1, D_MODEL), lambda i, j: (0, 0)),
                  pl.BlockSpec((D_MODEL, tn), lambda i, j: (0, j))],
        out_specs=pl.BlockSpec((tm, tn), lambda i, j: (i, j)),
        out_shape=jax.ShapeDtypeStruct((n_tok, IN_WIDTH), F32),
        scratch_shapes=[pltpu.VMEM((tm, D_MODEL), BF16)],
        compiler_params=_params("parallel", "arbitrary"),
    )(x, mod, gain.reshape(1, D_MODEL), w_bf)


def _ctx_na_kernel(q_ref, k_ref, v_ref, o_ref):
    scale = NA_DH ** -0.5
    for h in range(NA_HEADS):
        sl = slice(h * NA_DH, (h + 1) * NA_DH)
        s = _dot_nt(q_ref[:, sl], k_ref[:, sl]) * scale
        (p,) = _softmax_parts([s])
        o_ref[:, sl] = _dot(p, v_ref[:, sl])


def _ctx_na(z):
    blk = lambda j: pl.BlockSpec((SEQ, 512), lambda b: (b, j))
    return pl.pallas_call(
        _ctx_na_kernel,
        grid=(BATCH,),
        in_specs=[blk(COL_NA_Q // 512), blk(COL_NA_K // 512), blk(COL_NA_V // 512)],
        out_specs=pl.BlockSpec((SEQ, 512), lambda b: (b, 0)),
        out_shape=jax.ShapeDtypeStruct((BATCH * SEQ, BRANCH_W), F32),
        compiler_params=_params("parallel"),
    )(z, z, z)


def _diff_lambda(lam_ref, lam_init):
    a = jnp.sum(lam_ref[0:1, :] * lam_ref[1:2, :], axis=-1, keepdims=True)
    b = jnp.sum(lam_ref[2:3, :] * lam_ref[3:4, :], axis=-1, keepdims=True)
    return jnp.exp(a) - jnp.exp(b) + lam_init


def _ctx_diff_kernel(lam_ref, q_ref, k_ref, v_ref, dn_ref, o_ref, *, lam_init):
    scale = DIFF_DH ** -0.5
    lam = _diff_lambda(lam_ref, lam_init)
    w = 2 * DIFF_DH
    for h in range(DIFF_HEADS):
        ps = []
        for c in range(2):
            c0 = c * DIFF_HEADS * DIFF_DH + h * DIFF_DH
            sl = slice(c0, c0 + DIFF_DH)
            s = _dot_nt(q_ref[:, sl], k_ref[:, sl]) * scale
            ps.append(_softmax_parts([s])[0])
        a = ps[0] - lam * ps[1]
        o = _dot(a, v_ref[:, h * w:(h + 1) * w])
        o_ref[:, h * w:(h + 1) * w] = _rms(o, dn_ref[...]) * (1.0 - lam_init)


def _ctx_diff(z, lam_p, diff_norm, lam_init):
    blk = lambda j: pl.BlockSpec((SEQ, 512), lambda b: (b, j))
    return pl.pallas_call(
        functools.partial(_ctx_diff_kernel, lam_init=lam_init),
        grid=(BATCH,),
        in_specs=[pl.BlockSpec((4, DIFF_DH), lambda b: (0, 0)),
                  blk(COL_DF_Q // 512), blk(COL_DF_K // 512), blk(COL_DF_V // 512),
                  pl.BlockSpec((1, 2 * DIFF_DH), lambda b: (0, 0))],
        out_specs=pl.BlockSpec((SEQ, 512), lambda b: (b, 0)),
        out_shape=jax.ShapeDtypeStruct((BATCH * SEQ, BRANCH_W), F32),
        compiler_params=_params("parallel"),
    )(lam_p, z, z, z, diff_norm.reshape(1, 2 * DIFF_DH))


NA_GRID_ROWS = DEC_SEQ // GRID_W
NA_KEY_ROWS = min(NA_ROWS, NA_GRID_ROWS)
NA_LOCAL = NA_KEY_ROWS * GRID_W


def _na_bias_table(rpb):
    qc = np.arange(GRID_W)[:, None]
    kc = np.arange(GRID_W)[None, :]
    cstart = np.clip(qc - NA_COLS // 2, 0, GRID_W - NA_COLS)
    ok = (kc >= cstart) & (kc < cstart + NA_COLS)
    dc = np.clip(kc - qc + NA_COLS - 1, 0, 2 * NA_COLS - 2)
    dr = np.arange(NA_ROWS)[:, None] + np.arange(NA_KEY_ROWS)[None, :]
    b = rpb[:, dr[:, :, None, None], dc[None, None, :, :]]
    b = jnp.where(jnp.asarray(ok)[None, None, None], b.astype(F32), NEG_INF)
    return b.transpose(0, 1, 3, 2, 4).reshape(NA_HEADS, NA_ROWS, GRID_W, NA_LOCAL)


def _lat_na_kernel(q_ref, k_ref, v_ref, kc_ref, vc_ref, bias_ref, o_ref):
    scale = NA_DH ** -0.5

    def body(r, carry):
        rs = jnp.clip(r - NA_KEY_ROWS // 2, 0, NA_GRID_ROWS - NA_KEY_ROWS)
        d0 = rs - r + NA_ROWS - 1
        q0 = pl.multiple_of(r * GRID_W, GRID_W)
        k0 = pl.multiple_of(rs * GRID_W, GRID_W)
        outs = []
        for hh in range(2):
            sl = slice(hh * NA_DH, (hh + 1) * NA_DH)
            q = q_ref[pl.ds(q0, GRID_W), sl]
            s_loc = _dot_nt(q, k_ref[pl.ds(k0, NA_LOCAL), sl]) * scale + bias_ref[hh, d0]
            s_ctx = _dot_nt(q, kc_ref[0, 0, hh]) * scale
            p_loc, p_ctx = _softmax_parts([s_loc, s_ctx])
            outs.append(_dot(p_loc, v_ref[pl.ds(k0, NA_LOCAL), sl]) + _dot(p_ctx, vc_ref[0, 0, hh]))
        o_ref[pl.ds(q0, GRID_W), :] = jnp.concatenate(outs, axis=1)
        return carry

    lax.fori_loop(0, NA_GRID_ROWS, body, 0)


def _lat_na(z, cache_k, cache_v, bias, l):
    blk = lambda c0: pl.BlockSpec((DEC_SEQ, LANES), lambda b, p: (b, c0 // LANES + p))
    cblk = pl.BlockSpec((1, 1, 2, PAST_LEN, NA_DH), lambda b, p: (b, l, p, 0, 0))
    return pl.pallas_call(
        _lat_na_kernel,
        grid=(DEC_BATCH, NA_HEADS // 2),
        in_specs=[blk(COL_NA_Q), blk(COL_NA_K), blk(COL_NA_V), cblk, cblk,
                  pl.BlockSpec((2, NA_ROWS, GRID_W, NA_LOCAL), lambda b, p: (p, 0, 0, 0))],
        out_specs=pl.BlockSpec((DEC_SEQ, LANES), lambda b, p: (b, p)),
        out_shape=jax.ShapeDtypeStruct((DEC_BATCH * DEC_SEQ, BRANCH_W), F32),
        compiler_params=_params("parallel", "parallel"),
    )(z, z, z, cache_k, cache_v, bias)


def _rope_tables():
    t = jnp.arange(DEC_SEQ)
    quarter = DIFF_DH // 4
    inv = ROPE_BASE ** (-jnp.arange(quarter, dtype=F32) / quarter)
    ang_r = (t // GRID_W).astype(F32)[:, None] * inv
    ang_c = (t % GRID_W).astype(F32)[:, None] * inv
    lane = np.arange(LANES) % DIFF_DH
    idx = lane % quarter
    use_r = jnp.asarray(lane < DIFF_DH // 2)[None, :]
    first = jnp.asarray((lane % (2 * quarter)) < quarter)[None, :]
    ang = jnp.where(use_r, ang_r[:, idx], ang_c[:, idx])
    cos = jnp.cos(ang)
    sin = jnp.sin(ang)
    return cos, jnp.where(first, -sin, 0.0), jnp.where(first, 0.0, sin)


def _rope(x, cos, sin_a, sin_b):
    quarter = DIFF_DH // 4
    up = pltpu.roll(x, LANES - quarter, axis=1)
    down = pltpu.roll(x, quarter, axis=1)
    return x * cos + up * sin_a + down * sin_b


def _lat_diff_kernel(lam_ref, q_ref, k_ref, v_ref, kc_ref, vc_ref, cos_ref, sa_ref, sb_ref, dn_ref, o_ref,
                     krot_scr, vbf_scr, *, lam_init, tq):
    scale = DIFF_DH ** -0.5
    n_slab = 512 // LANES

    @pl.when(pl.program_id(1) == 0)
    def _():
        for j in range(n_slab):
            sl = slice(j * LANES, (j + 1) * LANES)
            krot_scr[:, sl] = _rope(k_ref[:, sl], cos_ref[...], sa_ref[...], sb_ref[...]).astype(BF16)
            vbf_scr[:, sl] = v_ref[:, sl].astype(BF16)

    r0 = pl.multiple_of(pl.program_id(1) * tq, tq)
    cos, sa, sb = cos_ref[pl.ds(r0, tq), :], sa_ref[pl.ds(r0, tq), :], sb_ref[pl.ds(r0, tq), :]
    qrot = jnp.concatenate([_rope(q_ref[:, j * LANES:(j + 1) * LANES], cos, sa, sb) for j in range(n_slab)],
                           axis=1).astype(BF16)
    lam = _diff_lambda(lam_ref, lam_init)
    w = 2 * DIFF_DH
    for h in range(DIFF_HEADS):
        ps = []
        for c in range(2):
            c0 = c * DIFF_HEADS * DIFF_DH + h * DIFF_DH
            qh = qrot[:, c0:c0 + DIFF_DH]
            s_loc = _dot_nt(qh, krot_scr[:, c0:c0 + DIFF_DH]) * scale
            s_ctx = _dot_nt(qh, kc_ref[0, 0, c, h]) * scale
            ps.append(_softmax_parts([s_loc, s_ctx]))
        a_loc = ps[0][0] - lam * ps[1][0]
        a_ctx = ps[0][1] - lam * ps[1][1]
        o = _dot(a_loc, vbf_scr[:, h * w:(h + 1) * w]) + _dot(a_ctx, vc_ref[0, 0, h])
        o_ref[:, h * w:(h + 1) * w] = _rms(o, dn_ref[...]) * (1.0 - lam_init)


def _lat_diff(z, cache_k, cache_v, lam_p, diff_norm, tables, lam_init, l):
    tq = 256
    nq = DEC_SEQ // tq
    tab = pl.BlockSpec((DEC_SEQ, LANES), lambda b, i: (0, 0))
    return pl.pallas_call(
        functools.partial(_lat_diff_kernel, lam_init=lam_init, tq=tq),
        grid=(DEC_BATCH, nq),
        in_specs=[pl.BlockSpec((4, DIFF_DH), lambda b, i: (0, 0)),
                  pl.BlockSpec((tq, 512), lambda b, i: (b * nq + i, COL_DF_Q // 512)),
                  pl.BlockSpec((DEC_SEQ, 512), lambda b, i: (b, COL_DF_K // 512)),
                  pl.BlockSpec((DEC_SEQ, 512), lambda b, i: (b, COL_DF_V // 512)),
                  pl.BlockSpec((1, 1, 2, DIFF_HEADS, PAST_LEN, DIFF_DH), lambda b, i: (b, l, 0, 0, 0, 0)),
                  pl.BlockSpec((1, 1, DIFF_HEADS, PAST_LEN, 2 * DIFF_DH), lambda b, i: (b, l, 0, 0, 0)),
                  tab, tab, tab,
                  pl.BlockSpec((1, 2 * DIFF_DH), lambda b, i: (0, 0))],
        out_specs=pl.BlockSpec((tq, 512), lambda b, i: (b * nq + i, 0)),
        out_shape=jax.ShapeDtypeStruct((DEC_BATCH * DEC_SEQ, BRANCH_W), F32),
        scratch_shapes=[pltpu.VMEM((DEC_SEQ, 512), BF16), pltpu.VMEM((DEC_SEQ, 512), BF16)],
        compiler_params=_params("parallel", "arbitrary"),
    )(lam_p, z, z, z, cache_k, cache_v, *tables, diff_norm.reshape(1, 2 * DIFF_DH))


def _ret_kernel(*refs, nc, has_state, write_state):
    th_ref, q_ref, k_ref, v_ref, g_ref, gn_ref = refs[:6]
    refs = refs[6:]
    if has_state:
        s0f_ref, s0b_ref = refs[:2]
        refs = refs[2:]
    y_ref = refs[0]
    refs = refs[1:]
    if write_state:
        sf_ref, sb_ref = refs[:2]
        refs = refs[2:]
    (of_scr,) = refs
    C = RET_CHUNK
    dh = RET_DH
    rel = (lax.broadcasted_iota(jnp.int32, (C, C), 0) - lax.broadcasted_iota(jnp.int32, (C, C), 1)).astype(F32)
    pos = lax.broadcasted_iota(jnp.int32, (C, dh), 0).astype(F32)
    for hh in range(2):
        sl = slice(hh * dh, (hh + 1) * dh)
        lg = jnp.log1p(-jnp.exp(th_ref[0, hh]))
        lgf, lgb = lg[0:1, :], lg[1:2, :]
        lgf_c = jnp.concatenate([lgf] * (C // dh), axis=1)
        lgb_c = jnp.concatenate([lgb] * (C // dh), axis=1)
        dmat_f = jnp.where(rel >= 0, jnp.exp(jnp.where(rel >= 0, rel, 0.0) * lgf_c), 0.0)
        dmat_b = jnp.where(rel < 0, jnp.exp(jnp.where(rel < 0, -rel, 0.0) * lgb_c), 0.0)
        xi_f = jnp.exp((pos + 1.0) * lgf)
        zeta_f = jnp.exp((C - 1.0 - pos) * lgf)
        gc_f = jnp.exp(C * lgf)
        xi_b = jnp.exp((C - pos) * lgb)
        zeta_b = jnp.exp(pos * lgb)
        gc_b = jnp.exp(C * lgb)
        gn = gn_ref[:, sl]

        def load(c):
            r0 = pl.multiple_of(c * C, C)
            rows = pl.ds(r0, C)
            return rows, q_ref[rows, sl], k_ref[rows, sl] * (dh ** -0.5), v_ref[rows, sl]

        def fwd(c, s):
            rows, q, k, v = load(c)
            inner = _dot_nt(q, k) * dmat_f
            of_scr[rows, sl] = _dot(inner, v) + _dot(q, s) * xi_f
            return gc_f * s + _dot_tn(k * zeta_f, v)

        def bwd(i, s):
            rows, q, k, v = load(nc - 1 - i)
            inner = _dot_nt(q, k) * dmat_b
            o = of_scr[rows, sl] + _dot(inner, v) + _dot(q, s) * xi_b
            g = g_ref[rows, sl]
            y_ref[rows, sl] = _rms(o, gn) * (g * _sigmoid(g))
            return gc_b * s + _dot_tn(k * zeta_b, v)

        if has_state:
            s0f, s0b = s0f_ref[0, hh], s0b_ref[0, hh]
        else:
            s0f = s0b = jnp.zeros((dh, dh), F32)
        s_f = lax.fori_loop(0, nc, fwd, s0f)
        s_b = lax.fori_loop(0, nc, bwd, s0b)
        if write_state:
            sf_ref[0, hh] = s_f
            sb_ref[0, hh] = s_b


def _retention(z, theta, ret_norm, n_batch, t_len, state=None):
    blk = lambda c0: pl.BlockSpec((t_len, LANES), lambda b, p: (b, c0 // LANES + p))
    sblk = pl.BlockSpec((1, 2, RET_DH, RET_DH), lambda b, p: (b, p, 0, 0))
    in_specs = [pl.BlockSpec((1, 2, 2, RET_DH), lambda b, p: (p, 0, 0, 0)),
                blk(COL_RT_Q), blk(COL_RT_K), blk(COL_RT_V), blk(COL_RT_G),
                pl.BlockSpec((1, LANES), lambda b, p: (0, p))]
    args = [theta, z, z, z, z, ret_norm.reshape(1, RET_HEADS * RET_DH)]
    y_spec = pl.BlockSpec((t_len, LANES), lambda b, p: (b, p))
    y_shape = jax.ShapeDtypeStruct((n_batch * t_len, BRANCH_W), F32)
    if state is not None:
        in_specs += [sblk, sblk]
        args += list(state)
        out_specs, out_shape = y_spec, y_shape
    else:
        s_shape = jax.ShapeDtypeStruct((n_batch, RET_HEADS, RET_DH, RET_DH), F32)
        out_specs, out_shape = [y_spec, sblk, sblk], [y_shape, s_shape, s_shape]
    return pl.pallas_call(
        functools.partial(_ret_kernel, nc=t_len // RET_CHUNK, has_state=state is not None,
                          write_state=state is None),
        grid=(n_batch, RET_HEADS // 2),
        in_specs=in_specs, out_specs=out_specs, out_shape=out_shape,
        scratch_shapes=[pltpu.VMEM((t_len, LANES), F32)],
        compiler_params=_params("parallel", "parallel"),
    )(*args)


def _shift_rows(x, k, fill, up):
    t_len = x.shape[0]
    if k % SUBLANES == 0:
        pad = jnp.full((k, x.shape[1]), fill, x.dtype)
        return jnp.concatenate([x[k:], pad], axis=0) if up else jnp.concatenate([pad, x[:t_len - k]], axis=0)
    row = lax.broadcasted_iota(jnp.int32, x.shape, 0)
    if up:
        return jnp.where(row < t_len - k, pltpu.roll(x, t_len - k, axis=0), fill)
    return jnp.where(row >= k, pltpu.roll(x, k, axis=0), fill)


def _linear_scan(a, u, up):
    t_len = a.shape[0]
    k = 1
    while k < t_len:
        u = a * _shift_rows(u, k, 0.0, up) + u
        if 2 * k < t_len:
            a = a * _shift_rows(a, k, 1.0, up)
        k *= 2
    return u


def _lru_kernel(*refs, has_state, write_state):
    x_ref, g_ref, vec_ref, w_ref = refs[:4]
    refs = refs[4:]
    if has_state:
        h0f_ref, h0b_ref = refs[:2]
        refs = refs[2:]
    y_ref = refs[0]
    if write_state:
        hf_ref, hb_ref = refs[1:3]
    x = x_ref[...]
    t_len = x.shape[0]
    row = lax.broadcasted_iota(jnp.int32, x.shape, 0)
    vec = lambda i: vec_ref[i:i + 1, :]
    xd = (vec(0) * _shift_rows(x, 1, 0.0, False) + vec(1) * x + vec(2) * _shift_rows(x, 1, 0.0, True)
          + vec(3) * _shift_rows(x, 2, 0.0, True) + vec(4))

    def gates(wa, wx, ba, bx, lam):
        r = _sigmoid(_dot(xd, wa) + ba)
        i = _sigmoid(_dot(xd, wx) + bx)
        nl = -lam
        softplus = jnp.maximum(nl, 0.0) + jnp.log1p(jnp.exp(-jnp.abs(nl)))
        log_a = -LRU_C * r * softplus
        a = jnp.exp(log_a)
        return a, jnp.sqrt(-jnp.tanh(log_a) * (a * a + 1.0)) * (i * xd)

    a, u = gates(w_ref[0, 0], w_ref[1, 0], vec(5), vec(6), vec(7))
    if has_state:
        u = u + jnp.where(row == 0, a * h0f_ref[0], 0.0)
    h_f = _linear_scan(a, u, False)
    a, u = gates(w_ref[2, 0], w_ref[3, 0], vec(8), vec(9), vec(10))
    if has_state:
        u = u + jnp.where(row == t_len - 1, a * h0b_ref[0], 0.0)
    h_b = _linear_scan(a, u, True)
    g = g_ref[...]
    gelu = 0.5 * g * (1.0 + jnp.tanh(math.sqrt(2.0 / math.pi) * (g + 0.044715 * (g * g * g))))
    y_ref[...] = (h_f + h_b) * gelu
    if write_state:
        hf_ref[0] = h_f[t_len - 1:t_len, :]
        hb_ref[0] = h_b[0:1, :]


def _rglru(z, vecs, w_gate, n_batch, t_len, state=None):
    blk = lambda c0: pl.BlockSpec((t_len, LANES), lambda b, j: (b, c0 // LANES + j))
    hblk = pl.BlockSpec((1, 1, LANES), lambda b, j: (b, 0, j))
    in_specs = [blk(COL_LR_X), blk(COL_LR_G),
                pl.BlockSpec((11, LANES), lambda b, j: (0, j)),
                pl.BlockSpec((4, 1, LANES, LANES), lambda b, j: (0, j, 0, 0))]
    args = [z, z, vecs, w_gate]
    y_spec = pl.BlockSpec((t_len, LANES), lambda b, j: (b, j))
    y_shape = jax.ShapeDtypeStruct((n_batch * t_len, BRANCH_W), F32)
    if state is not None:
        in_specs += [hblk, hblk]
        args += [s.reshape(n_batch, 1, LRU_WIDTH) for s in state]
        out_specs, out_shape = y_spec, y_shape
    else:
        h_shape = jax.ShapeDtypeStruct((n_batch, 1, LRU_WIDTH), F32)
        out_specs, out_shape = [y_spec, hblk, hblk], [y_shape, h_shape, h_shape]
    return pl.pallas_call(
        functools.partial(_lru_kernel, has_state=state is not None, write_state=state is None),
        grid=(n_batch, LRU_WIDTH // LANES),
        in_specs=in_specs, out_specs=out_specs, out_shape=out_shape,
        compiler_params=_params("parallel", "parallel"),
    )(*args)


def _pair_block_diag(w):
    bw = LRU_WIDTH // LRU_BLOCKS
    w = w.reshape(LRU_BLOCKS // 2, 2, bw, bw)
    zero = jnp.zeros_like(w[:, 0])
    top = jnp.concatenate([w[:, 0], zero], axis=2)
    bot = jnp.concatenate([zero, w[:, 1]], axis=2)
    return jnp.concatenate([top, bot], axis=1)


def _merge_kernel(x_ref, ya_ref, yb_ref, yc_ref, yd_ref, g0_ref, g1_ref, g2_ref, g3_ref,
                  wb_ref, wo_ref, n_ref, m_ref, o_ref):
    acc = None
    for k, (y_ref, g_ref) in enumerate(((ya_ref, g0_ref), (yb_ref, g1_ref), (yc_ref, g2_ref), (yd_ref, g3_ref))):
        term = _sigmoid(g_ref[...]) * jnp.dot(y_ref[...].astype(BF16), wb_ref[k], preferred_element_type=F32)
        acc = term if acc is None else acc + term
    out = jnp.dot(acc.astype(BF16), wo_ref[...], preferred_element_type=F32)
    o_ref[...] = x_ref[...] + m_ref[0, 2:3, :] * _rms(out, n_ref[...])


def _merge(x, ys, z, mod, wb_bf, wo_bf, gain, mod_row):
    n_tok = x.shape[0]
    tm = 256
    yblk = pl.BlockSpec((tm, BRANCH_W), lambda i: (i, 0))
    gblk = lambda k: pl.BlockSpec((tm, D_MODEL), lambda i: (i, COL_GATE // D_MODEL + k))
    return pl.pallas_call(
        _merge_kernel,
        grid=(n_tok // tm,),
        in_specs=[pl.BlockSpec((tm, D_MODEL), lambda i: (i, 0)), yblk, yblk, yblk, yblk,
                  gblk(0), gblk(1), gblk(2), gblk(3),
                  pl.BlockSpec((N_BRANCH, BRANCH_W, D_MODEL), lambda i: (0, 0, 0)),
                  pl.BlockSpec((D_MODEL, D_MODEL), lambda i: (0, 0)),
                  pl.BlockSpec((1, D_MODEL), lambda i: (0, 0)),
                  pl.BlockSpec((1, 6, D_MODEL), lambda i: (mod_row(i, tm), 0, 0))],
        out_specs=pl.BlockSpec((tm, D_MODEL), lambda i: (i, 0)),
        out_shape=jax.ShapeDtypeStruct((n_tok, D_MODEL), F32),
        compiler_params=_params("parallel"),
    )(x, *ys, z, z, z, z, wb_bf, wo_bf, gain.reshape(1, D_MODEL), mod)


def _mlp_kernel(x_ref, m_ref, gpre_ref, gpost_ref, w1_ref, w2_ref, o_ref):
    x = x_ref[...]
    h = (_rms(x, gpre_ref[...]) * (1.0 + m_ref[0, 4:5, :]) + m_ref[0, 3:4, :]).astype(BF16)
    ff_chunk = D_MODEL
    y = None
    for j in range(D_FF // ff_chunk):
        sl = slice(j * ff_chunk, (j + 1) * ff_chunk)
        a = jnp.maximum(jnp.dot(h, w1_ref[:, sl], preferred_element_type=F32), 0.0)
        part = jnp.dot((a * a).astype(BF16), w2_ref[sl, :], preferred_element_type=F32)
        y = part if y is None else y + part
    o_ref[...] = x + m_ref[0, 5:6, :] * _rms(y, gpost_ref[...])


def _mlp(x, mod, gpre, gpost, w1_bf, w2_bf, mod_row):
    n_tok = x.shape[0]
    tm = 256
    return pl.pallas_call(
        _mlp_kernel,
        grid=(n_tok // tm,),
        in_specs=[pl.BlockSpec((tm, D_MODEL), lambda i: (i, 0)),
                  pl.BlockSpec((1, 6, D_MODEL), lambda i: (mod_row(i, tm), 0, 0)),
                  pl.BlockSpec((1, D_MODEL), lambda i: (0, 0)),
                  pl.BlockSpec((1, D_MODEL), lambda i: (0, 0)),
                  pl.BlockSpec((D_MODEL, D_FF), lambda i: (0, 0)),
                  pl.BlockSpec((D_FF, D_MODEL), lambda i: (0, 0))],
        out_specs=pl.BlockSpec((tm, D_MODEL), lambda i: (i, 0)),
        out_shape=jax.ShapeDtypeStruct((n_tok, D_MODEL), F32),
        compiler_params=_params("parallel"),
    )(x, mod, gpre.reshape(1, D_MODEL), gpost.reshape(1, D_MODEL), w1_bf, w2_bf)


def _ctx_mod_row(i, tm):
    return 0


def _lat_mod_row(i, tm):
    return 1 + i // (DEC_SEQ // tm)


def kernel(x_prompt, x_sample, cache_na_k, cache_na_v, cache_diff_k, cache_diff_v, state_ret_fwd, state_ret_bwd, state_lru_fwd, state_lru_bwd, c, c_ctx, ada_w, ada_b, norm_mix_pre, norm_mix_post, norm_ffn_pre, norm_ffn_post, w_in, na_rpb, diff_lq1, diff_lk1, diff_lq2, diff_lk2, diff_norm, ret_theta_fwd, ret_theta_bwd, ret_norm, lru_conv_w, lru_conv_b, lru_wa_fwd, lru_ba_fwd, lru_wx_fwd, lru_bx_fwd, lru_lam_fwd, lru_wa_bwd, lru_ba_bwd, lru_wx_bwd, lru_bx_bwd, lru_lam_bwd, w_branch, w_out, mlp_w1, mlp_w2):
    xc = x_prompt.reshape(BATCH * SEQ, D_MODEL)
    xl = x_sample.reshape(DEC_BATCH * DEC_SEQ, D_MODEL)
    cond = jnp.concatenate([c_ctx[None, :], c, jnp.zeros((COND_PAD - N_COND, D_MODEL), F32)], axis=0)
    mods = _modulation(cond.T, ada_w, ada_b).reshape(DEPTH, COND_PAD, 6, D_MODEL)
    rope_tables = _rope_tables()
    outs = [[] for _ in range(8)]
    for l in range(DEPTH):
        mod = mods[l]
        lam_init = 0.8 - 0.6 * math.exp(-0.3 * l)
        w_in_bf = w_in[l].astype(BF16)
        wb_bf = w_branch[l].astype(BF16)
        wo_bf = w_out[l].astype(BF16)
        w1_bf = mlp_w1[l].astype(BF16)
        w2_bf = mlp_w2[l].astype(BF16)
        lam_p = jnp.stack([diff_lq1[l], diff_lk1[l], diff_lq2[l], diff_lk2[l]])
        theta = jnp.broadcast_to(jnp.stack([ret_theta_fwd[l], ret_theta_bwd[l]], axis=1)[:, :, None],
                                 (RET_HEADS, 2, RET_DH)).reshape(RET_HEADS // 2, 2, 2, RET_DH)
        vecs = jnp.concatenate([lru_conv_w[l], lru_conv_b[l][None],
                                lru_ba_fwd[l][None], lru_bx_fwd[l][None], lru_lam_fwd[l][None],
                                lru_ba_bwd[l][None], lru_bx_bwd[l][None], lru_lam_bwd[l][None]], axis=0)
        w_gate = jnp.stack([_pair_block_diag(w) for w in
                            (lru_wa_fwd[l], lru_wx_fwd[l], lru_wa_bwd[l], lru_wx_bwd[l])])
        bias = _na_bias_table(na_rpb[l])

        zc = _inproj(xc, mod, norm_mix_pre[l], w_in_bf, _ctx_mod_row)
        ya = _ctx_na(zc)
        yb = _ctx_diff(zc, lam_p, diff_norm[l], lam_init)
        yc, s_rf, s_rb = _retention(zc, theta, ret_norm[l], BATCH, SEQ)
        yd, h_lf, h_lb = _rglru(zc, vecs, w_gate, BATCH, SEQ)
        xc = _merge(xc, (ya, yb, yc, yd), zc, mod, wb_bf, wo_bf, norm_mix_post[l], _ctx_mod_row)
        xc = _mlp(xc, mod, norm_ffn_pre[l], norm_ffn_post[l], w1_bf, w2_bf, _ctx_mod_row)
        z4 = zc.reshape(BATCH, SEQ, IN_WIDTH)
        heads = lambda c0, n, d: z4[:, :, c0:c0 + n * d].reshape(BATCH, SEQ, n, d).transpose(0, 2, 1, 3)
        outs[0].append(heads(COL_NA_K, NA_HEADS, NA_DH))
        outs[1].append(heads(COL_NA_V, NA_HEADS, NA_DH))
        outs[2].append(heads(COL_DF_K, 2 * DIFF_HEADS, DIFF_DH).reshape(BATCH, 2, DIFF_HEADS, SEQ, DIFF_DH))
        outs[3].append(heads(COL_DF_V, DIFF_HEADS, 2 * DIFF_DH))
        outs[4].append(s_rf)
        outs[5].append(s_rb)
        outs[6].append(h_lf.reshape(BATCH, LRU_WIDTH))
        outs[7].append(h_lb.reshape(BATCH, LRU_WIDTH))

        zl = _inproj(xl, mod, norm_mix_pre[l], w_in_bf, _lat_mod_row)
        ya = _lat_na(zl, cache_na_k, cache_na_v, bias, l)
        yb = _lat_diff(zl, cache_diff_k, cache_diff_v, lam_p, diff_norm[l], rope_tables, lam_init, l)
        yc = _retention(zl, theta, ret_norm[l], DEC_BATCH, DEC_SEQ,
                        state=(state_ret_fwd[:, l], state_ret_bwd[:, l]))
        yd = _rglru(zl, vecs, w_gate, DEC_BATCH, DEC_SEQ, state=(state_lru_fwd[:, l], state_lru_bwd[:, l]))
        xl = _merge(xl, (ya, yb, yc, yd), zl, mod, wb_bf, wo_bf, norm_mix_post[l], _lat_mod_row)
        xl = _mlp(xl, mod, norm_ffn_pre[l], norm_ffn_post[l], w1_bf, w2_bf, _lat_mod_row)

    stacked = [jnp.stack(o, axis=1) for o in outs]
    return (xc.reshape(BATCH, SEQ, D_MODEL), xl.reshape(DEC_BATCH, DEC_SEQ, D_MODEL), *stacked)
```

```python
import functools
import math

import numpy as np
import jax
import jax.numpy as jnp
from jax import lax
from jax.experimental import pallas as pl
from jax.experimental.pallas import tpu as pltpu

F32 = jnp.float32
BF16 = jnp.bfloat16

D_MODEL = 1024
BATCH = 32
SEQ = 256
DEPTH = 4
DEC_BATCH = 4
DEC_SEQ = 2048
PAST_LEN = 512
GRID_W = 64
NA_HEADS = 8
NA_DH = 64
NA_ROWS = 8
NA_COLS = 16
DIFF_HEADS = 4
DIFF_DH = 64
RET_HEADS = 8
RET_DH = 64
LRU_WIDTH = 512
LRU_BLOCKS = 8
LRU_CONV = 4
LRU_C = 8.0
N_BRANCH = 4
BRANCH_W = 512
D_FF = 4 * D_MODEL
ROPE_BASE = 10000.0
EPS = 1e-6
NEG_INF = -1e30

COL_NA_Q, COL_NA_K, COL_NA_V = 0, 512, 1024
COL_DF_Q, COL_DF_K, COL_DF_V = 1536, 2048, 2560
COL_RT_Q, COL_RT_K, COL_RT_V, COL_RT_G = 3072, 3584, 4096, 4608
COL_LR_X, COL_LR_G = 5120, 5632
COL_GATE = 6144
IN_WIDTH = COL_GATE + N_BRANCH * D_MODEL

N_COND = 1 + DEC_BATCH
COND_PAD = 8
LANES = 128
SUBLANES = 8
VMEM_LIMIT = 56 * 1024 * 1024


def _params(*sem):
    return pltpu.CompilerParams(dimension_semantics=sem, vmem_limit_bytes=VMEM_LIMIT)


def _dot(a, b):
    return jnp.dot(a.astype(BF16), b.astype(BF16), preferred_element_type=F32)


def _dot_nt(a, b):
    return lax.dot_general(a.astype(BF16), b.astype(BF16), (((1,), (1,)), ((), ())),
                           preferred_element_type=F32)


def _dot_tn(a, b):
    return lax.dot_general(a.astype(BF16), b.astype(BF16), (((0,), (0,)), ((), ())),
                           preferred_element_type=F32)


def _rms(x, g):
    return x * lax.rsqrt(jnp.mean(x * x, axis=-1, keepdims=True) + EPS) * g


def _sigmoid(x):
    return 1.0 / (1.0 + jnp.exp(-x))


def _softmax_parts(parts):
    m = functools.reduce(jnp.maximum, [jnp.max(s, axis=-1, keepdims=True) for s in parts])
    es = [jnp.exp(s - m) for s in parts]
    inv = 1.0 / functools.reduce(lambda a, b: a + b, [jnp.sum(e, axis=-1, keepdims=True) for e in es])
    return [e * inv for e in es]


def _mod_kernel(ct_ref, w_ref, b_ref, o_ref):
    ct = ct_ref[...]
    s = ct * _sigmoid(ct)
    w = w_ref[0]
    rows = [jnp.sum(s[:, r:r + 1] * w, axis=0, keepdims=True) + b_ref[0] for r in range(N_COND)]
    rows.append(jnp.zeros((COND_PAD - N_COND, w.shape[1]), F32))
    o_ref[0] = jnp.concatenate(rows, axis=0)


def _modulation(cond_t, ada_w, ada_b):
    tn = 768
    n = 6 * D_MODEL
    return pl.pallas_call(
        _mod_kernel,
        grid=(DEPTH, n // tn),
        in_specs=[pl.BlockSpec((D_MODEL, COND_PAD), lambda l, j: (0, 0)),
                  pl.BlockSpec((1, D_MODEL, tn), lambda l, j: (l, 0, j)),
                  pl.BlockSpec((1, 1, tn), lambda l, j: (l, 0, j))],
        out_specs=pl.BlockSpec((1, COND_PAD, tn), lambda l, j: (l, 0, j)),
        out_shape=jax.ShapeDtypeStruct((DEPTH, COND_PAD, n), F32),
        compiler_params=_params("parallel", "parallel"),
        name="modulation",
    )(cond_t, ada_w, ada_b.reshape(DEPTH, 1, n))


def _inproj_kernel(x_ref, m_ref, g_ref, w_ref, z_ref, h_scr):
    @pl.when(pl.program_id(1) == 0)
    def _():
        h = _rms(x_ref[...], g_ref[...]) * (1.0 + m_ref[0, 1:2, :]) + m_ref[0, 0:1, :]
        h_scr[...] = h.astype(BF16)

    z_ref[...] = jnp.dot(h_scr[...], w_ref[...], preferred_element_type=F32)


def _inproj(x, mod, gain, w_bf, mod_row):
    n_tok = x.shape[0]
    tm, tn = 1024, 1024
    return pl.pallas_call(
        _inproj_kernel,
        grid=(n_tok // tm, IN_WIDTH // tn),
        in_specs=[pl.BlockSpec((tm, D_MODEL), lambda i, j: (i, 0)),
                  pl.BlockSpec((1, 6, D_MODEL), lambda i, j: (mod_row(i, tm), 0, 0)),
                  pl.BlockSpec((1, D_MODEL), lambda i, j: (0, 0)),
                  pl.BlockSpec((D_MODEL, tn), lambda i, j: (0, j))],
        out_specs=pl.BlockSpec((tm, tn), lambda i, j: (i, j)),
        out_shape=jax.ShapeDtypeStruct((n_tok, IN_WIDTH), F32),
        scratch_shapes=[pltpu.VMEM((tm, D_MODEL), BF16)],
        compiler_params=_params("parallel", "arbitrary"),
        name="inproj",
    )(x, mod, gain.reshape(1, D_MODEL), w_bf)


def _ctx_na_kernel(q_ref, k_ref, v_ref, o_ref):
    scale = NA_DH ** -0.5
    for h in range(NA_HEADS):
        sl = slice(h * NA_DH, (h + 1) * NA_DH)
        s = _dot_nt(q_ref[:, sl] * scale, k_ref[:, sl])
        (p,) = _softmax_parts([s])
        o_ref[:, sl] = _dot(p, v_ref[:, sl])


def _ctx_na(z):
    blk = lambda j: pl.BlockSpec((SEQ, 512), lambda b: (b, j))
    return pl.pallas_call(
        _ctx_na_kernel,
        grid=(BATCH,),
        in_specs=[blk(COL_NA_Q // 512), blk(COL_NA_K // 512), blk(COL_NA_V // 512)],
        out_specs=pl.BlockSpec((SEQ, 512), lambda b: (b, 0)),
        out_shape=jax.ShapeDtypeStruct((BATCH * SEQ, BRANCH_W), F32),
        compiler_params=_params("parallel"),
        name="ctx_na",
    )(z, z, z)


def _diff_lambda(lam_ref, lam_init):
    a = jnp.sum(lam_ref[0:1, :] * lam_ref[1:2, :], axis=-1, keepdims=True)
    b = jnp.sum(lam_ref[2:3, :] * lam_ref[3:4, :], axis=-1, keepdims=True)
    return jnp.exp(a) - jnp.exp(b) + lam_init


def _ctx_diff_kernel(lam_ref, q_ref, k_ref, v_ref, dn_ref, o_ref, *, lam_init):
    scale = DIFF_DH ** -0.5
    lam = _diff_lambda(lam_ref, lam_init)
    w = 2 * DIFF_DH
    for h in range(DIFF_HEADS):
        ps = []
        for c in range(2):
            c0 = c * DIFF_HEADS * DIFF_DH + h * DIFF_DH
            sl = slice(c0, c0 + DIFF_DH)
            s = _dot_nt(q_ref[:, sl] * scale, k_ref[:, sl])
            ps.append(_softmax_parts([s])[0])
        a = ps[0] - lam * ps[1]
        o = _dot(a, v_ref[:, h * w:(h + 1) * w])
        o_ref[:, h * w:(h + 1) * w] = _rms(o, dn_ref[...]) * (1.0 - lam_init)


def _ctx_diff(z, lam_p, diff_norm, lam_init):
    blk = lambda j: pl.BlockSpec((SEQ, 512), lambda b: (b, j))
    return pl.pallas_call(
        functools.partial(_ctx_diff_kernel, lam_init=lam_init),
        grid=(BATCH,),
        in_specs=[pl.BlockSpec((4, DIFF_DH), lambda b: (0, 0)),
                  blk(COL_DF_Q // 512), blk(COL_DF_K // 512), blk(COL_DF_V // 512),
                  pl.BlockSpec((1, 2 * DIFF_DH), lambda b: (0, 0))],
        out_specs=pl.BlockSpec((SEQ, 512), lambda b: (b, 0)),
        out_shape=jax.ShapeDtypeStruct((BATCH * SEQ, BRANCH_W), F32),
        compiler_params=_params("parallel"),
        name="ctx_diff",
    )(lam_p, z, z, z, diff_norm.reshape(1, 2 * DIFF_DH))


NA_GRID_ROWS = DEC_SEQ // GRID_W
NA_KEY_ROWS = min(NA_ROWS, NA_GRID_ROWS)


NA_DR = 2 * NA_ROWS - 1
NA_DC = 2 * NA_COLS - 1


def _na_col_kernel(rpb_ref, idx_ref, o_ref):
    idx = jnp.broadcast_to(idx_ref[...], o_ref.shape)
    acc = jnp.full(o_ref.shape, NEG_INF, F32)
    for d in range(NA_DC):
        acc = jnp.where(idx == d, jnp.broadcast_to(rpb_ref[:, d:d + 1], o_ref.shape), acc)
    o_ref[...] = acc


def _na_col_tables(na_rpb):
    qc = np.arange(GRID_W)[:, None]
    kc = np.arange(GRID_W)[None, :]
    cstart = np.clip(qc - NA_COLS // 2, 0, GRID_W - NA_COLS)
    ok = (kc >= cstart) & (kc < cstart + NA_COLS)
    idx = np.where(ok, kc - qc + NA_COLS - 1, -1).astype(np.int32).reshape(1, GRID_W * GRID_W)
    rows = NA_HEADS * NA_DR
    rpb = jnp.pad(na_rpb.reshape(DEPTH * rows, NA_DC), ((0, 0), (0, 32 - NA_DC)))
    t = pl.pallas_call(
        _na_col_kernel,
        grid=(DEPTH,),
        in_specs=[pl.BlockSpec((rows, 32), lambda l: (l, 0)),
                  pl.BlockSpec((1, GRID_W * GRID_W), lambda l: (0, 0))],
        out_specs=pl.BlockSpec((rows, GRID_W * GRID_W), lambda l: (l, 0)),
        out_shape=jax.ShapeDtypeStruct((DEPTH * rows, GRID_W * GRID_W), F32),
        compiler_params=_params("parallel"),
        name="na_col_tables",
    )(rpb, jnp.asarray(idx))
    return t.reshape(DEPTH, NA_HEADS, NA_DR, GRID_W, GRID_W)


NA_QROWS = 4
NA_UROWS = 12
NA_QBLK = NA_QROWS * GRID_W
NA_UKEYS = NA_UROWS * GRID_W
NA_NBLK = NA_GRID_ROWS // NA_QROWS


def _na_union_start(r0):
    return int(np.clip(r0 - NA_KEY_ROWS // 2, 0, NA_GRID_ROWS - NA_UROWS))


def _na_bias_blocks(t):
    neg = jnp.full((NA_HEADS, GRID_W, GRID_W), NEG_INF, F32)
    variants = []
    for r0 in (0, NA_QROWS, NA_GRID_ROWS - NA_QROWS):
        us = _na_union_start(r0)
        q_rows = []
        for j in range(NA_QROWS):
            r = r0 + j
            rs = int(np.clip(r - NA_KEY_ROWS // 2, 0, NA_GRID_ROWS - NA_KEY_ROWS))
            blocks = [t[:, us + i - r + NA_ROWS - 1] if rs <= us + i < rs + NA_KEY_ROWS else neg
                      for i in range(NA_UROWS)]
            q_rows.append(jnp.concatenate(blocks, axis=-1))
        variants.append(jnp.concatenate(q_rows, axis=-2))
    return jnp.stack(variants)


def _lat_na_kernel(q_ref, k_ref, v_ref, kc_ref, vc_ref, bias_ref, o_ref):
    scale = NA_DH ** -0.5

    def body(blk, carry):
        us = jnp.clip(blk * NA_QROWS - NA_KEY_ROWS // 2, 0, NA_GRID_ROWS - NA_UROWS)
        variant = jnp.where(blk == 0, 0, jnp.where(blk == NA_NBLK - 1, 2, 1))
        q0 = pl.multiple_of(blk * NA_QBLK, NA_QBLK)
        k0 = pl.multiple_of(us * GRID_W, GRID_W)
        outs = []
        for hh in range(2):
            sl = slice(hh * NA_DH, (hh + 1) * NA_DH)
            q = (q_ref[pl.ds(q0, NA_QBLK), sl] * scale).astype(BF16)
            s_loc = _dot_nt(q, k_ref[pl.ds(k0, NA_UKEYS), sl]) + bias_ref[variant, hh]
            s_ctx = _dot_nt(q, kc_ref[0, 0, hh])
            p_loc, p_ctx = _softmax_parts([s_loc, s_ctx])
            outs.append(_dot(p_loc, v_ref[pl.ds(k0, NA_UKEYS), sl]) + _dot(p_ctx, vc_ref[0, 0, hh]))
        o_ref[pl.ds(q0, NA_QBLK), :] = jnp.concatenate(outs, axis=1)
        return carry

    lax.fori_loop(0, NA_NBLK, body, 0)


def _lat_na(z, cache_k, cache_v, bias, l):
    blk = lambda c0: pl.BlockSpec((DEC_SEQ, LANES), lambda p, b: (b, c0 // LANES + p))
    cblk = pl.BlockSpec((1, 1, 2, PAST_LEN, NA_DH), lambda p, b: (b, l, p, 0, 0))
    return pl.pallas_call(
        _lat_na_kernel,
        grid=(NA_HEADS // 2, DEC_BATCH),
        in_specs=[blk(COL_NA_Q), blk(COL_NA_K), blk(COL_NA_V), cblk, cblk,
                  pl.BlockSpec((3, 2, NA_QBLK, NA_UKEYS), lambda p, b: (0, p, 0, 0))],
        out_specs=pl.BlockSpec((DEC_SEQ, LANES), lambda p, b: (b, p)),
        out_shape=jax.ShapeDtypeStruct((DEC_BATCH * DEC_SEQ, BRANCH_W), F32),
        compiler_params=_params("parallel", "parallel"),
        name="lat_na",
    )(z, z, z, cache_k, cache_v, bias)


def _rope_tables():
    t = jnp.arange(DEC_SEQ)
    quarter = DIFF_DH // 4
    inv = ROPE_BASE ** (-jnp.arange(quarter, dtype=F32) / quarter)
    ang_r = (t // GRID_W).astype(F32)[:, None] * inv
    ang_c = (t % GRID_W).astype(F32)[:, None] * inv
    lane = np.arange(LANES) % DIFF_DH
    idx = lane % quarter
    use_r = jnp.asarray(lane < DIFF_DH // 2)[None, :]
    first = jnp.asarray((lane % (2 * quarter)) < quarter)[None, :]
    ang = jnp.where(use_r, ang_r[:, idx], ang_c[:, idx])
    cos = jnp.cos(ang)
    sin = jnp.sin(ang)
    return cos, jnp.where(first, -sin, 0.0), jnp.where(first, 0.0, sin)


def _rope(x, cos, sin_a, sin_b):
    quarter = DIFF_DH // 4
    up = pltpu.roll(x, LANES - quarter, axis=1)
    down = pltpu.roll(x, quarter, axis=1)
    return x * cos + up * sin_a + down * sin_b


def _lat_diff_kernel(lam_ref, q_ref, k_ref, v_ref, kc_ref, vc_ref, cos_ref, sa_ref, sb_ref, dn_ref, o_ref,
                     krot_scr, vbf_scr, *, lam_init, tq):
    scale = DIFF_DH ** -0.5
    n_slab = 512 // LANES

    @pl.when(pl.program_id(1) == 0)
    def _():
        for j in range(n_slab):
            sl = slice(j * LANES, (j + 1) * LANES)
            krot_scr[:, sl] = _rope(k_ref[:, sl], cos_ref[...], sa_ref[...], sb_ref[...]).astype(BF16)
            vbf_scr[:, sl] = v_ref[:, sl].astype(BF16)

    r0 = pl.multiple_of(pl.program_id(1) * tq, tq)
    cos, sa, sb = cos_ref[pl.ds(r0, tq), :], sa_ref[pl.ds(r0, tq), :], sb_ref[pl.ds(r0, tq), :]
    qrot = jnp.concatenate([_rope(q_ref[:, j * LANES:(j + 1) * LANES], cos, sa, sb) * scale for j in range(n_slab)],
                           axis=1).astype(BF16)
    lam = _diff_lambda(lam_ref, lam_init)
    w = 2 * DIFF_DH
    for h in range(DIFF_HEADS):
        ps = []
        for c in range(2):
            c0 = c * DIFF_HEADS * DIFF_DH + h * DIFF_DH
            qh = qrot[:, c0:c0 + DIFF_DH]
            s_loc = _dot_nt(qh, krot_scr[:, c0:c0 + DIFF_DH])
            s_ctx = _dot_nt(qh, kc_ref[0, 0, c, h])
            ps.append(_softmax_parts([s_loc, s_ctx]))
        a_loc = ps[0][0] - lam * ps[1][0]
        a_ctx = ps[0][1] - lam * ps[1][1]
        o = _dot(a_loc, vbf_scr[:, h * w:(h + 1) * w]) + _dot(a_ctx, vc_ref[0, 0, h])
        o_ref[:, h * w:(h + 1) * w] = _rms(o, dn_ref[...]) * (1.0 - lam_init)


def _lat_diff(z, cache_k, cache_v, lam_p, diff_norm, tables, lam_init, l):
    tq = 256
    nq = DEC_SEQ // tq
    tab = pl.BlockSpec((DEC_SEQ, LANES), lambda b, i: (0, 0))
    return pl.pallas_call(
        functools.partial(_lat_diff_kernel, lam_init=lam_init, tq=tq),
        grid=(DEC_BATCH, nq),
        in_specs=[pl.BlockSpec((4, DIFF_DH), lambda b, i: (0, 0)),
                  pl.BlockSpec((tq, 512), lambda b, i: (b * nq + i, COL_DF_Q // 512)),
                  pl.BlockSpec((DEC_SEQ, 512), lambda b, i: (b, COL_DF_K // 512)),
                  pl.BlockSpec((DEC_SEQ, 512), lambda b, i: (b, COL_DF_V // 512)),
                  pl.BlockSpec((1, 1, 2, DIFF_HEADS, PAST_LEN, DIFF_DH), lambda b, i: (b, l, 0, 0, 0, 0)),
                  pl.BlockSpec((1, 1, DIFF_HEADS, PAST_LEN, 2 * DIFF_DH), lambda b, i: (b, l, 0, 0, 0)),
                  tab, tab, tab,
                  pl.BlockSpec((1, 2 * DIFF_DH), lambda b, i: (0, 0))],
        out_specs=pl.BlockSpec((tq, 512), lambda b, i: (b * nq + i, 0)),
        out_shape=jax.ShapeDtypeStruct((DEC_BATCH * DEC_SEQ, BRANCH_W), F32),
        scratch_shapes=[pltpu.VMEM((DEC_SEQ, 512), BF16), pltpu.VMEM((DEC_SEQ, 512), BF16)],
        compiler_params=_params("parallel", "arbitrary"),
        name="lat_diff",
    )(lam_p, z, z, z, cache_k, cache_v, *tables, diff_norm.reshape(1, 2 * DIFF_DH))


RET_BLOCK = 256


def _ret_kernel(*refs, n_chunks, chained):
    th_ref, q_ref, k_ref, v_ref, g_ref, gn_ref = refs[:6]
    if chained:
        s0f_ref, s0b_ref, y_ref = refs[6:]
    else:
        y_ref, sf_ref, sb_ref = refs[6:]
    C = RET_BLOCK
    dh = RET_DH
    rel = (lax.broadcasted_iota(jnp.int32, (C, C), 0) - lax.broadcasted_iota(jnp.int32, (C, C), 1)).astype(F32)
    pos = lax.broadcasted_iota(jnp.int32, (C, dh), 0).astype(F32)
    for hh in range(2):
        sl = slice(hh * dh, (hh + 1) * dh)
        lg = jnp.log1p(-jnp.exp(th_ref[0, hh]))
        lgf, lgb = lg[0:1, :], lg[1:2, :]
        lgf_c = jnp.concatenate([lgf] * (C // dh), axis=1)
        lgb_c = jnp.concatenate([lgb] * (C // dh), axis=1)
        decay = jnp.exp(jnp.where(rel >= 0, rel * lgf_c, -rel * lgb_c))
        xi_f = jnp.exp((pos + 1.0) * lgf)
        zeta_f = jnp.exp((C - 1.0 - pos) * lgf)
        xi_b = jnp.exp((C - pos) * lgb)
        zeta_b = jnp.exp(pos * lgb)
        gn = gn_ref[:, sl]
        rows = [pl.ds(c * C, C) for c in range(n_chunks)]
        ks = [k_ref[r, sl] * (dh ** -0.5) for r in rows]
        vs = [v_ref[r, sl] for r in rows]
        uf = [_dot_tn(k * zeta_f, v) for k, v in zip(ks, vs)]
        ub = [_dot_tn(k * zeta_b, v) for k, v in zip(ks, vs)]
        if chained:
            gc_f = jnp.exp(C * lgf)
            gc_b = jnp.exp(C * lgb)
            s_f = [s0f_ref[0, hh]]
            for c in range(n_chunks - 1):
                s_f.append(gc_f * s_f[c] + uf[c])
            s_b = [s0b_ref[0, hh]]
            for c in range(n_chunks - 1, 0, -1):
                s_b.append(gc_b * s_b[-1] + ub[c])
            s_b = s_b[::-1]
        for c in range(n_chunks):
            q = q_ref[rows[c], sl]
            o = _dot(_dot_nt(q, ks[c]) * decay, vs[c])
            if chained:
                o = o + _dot(q, s_f[c]) * xi_f + _dot(q, s_b[c]) * xi_b
            else:
                sf_ref[c, hh] = uf[c]
                sb_ref[c, hh] = ub[c]
            g = g_ref[rows[c], sl]
            y_ref[rows[c], sl] = _rms(o, gn) * (g * _sigmoid(g))


def _retention(z, theta, ret_norm, n_batch, t_len, state=None):
    chained = state is not None
    n_chunks = t_len // RET_BLOCK if chained else 4
    rows = n_chunks * RET_BLOCK
    n_steps = n_batch * t_len // rows
    blk = lambda c0: pl.BlockSpec((rows, LANES), lambda b, p: (b, c0 // LANES + p))
    in_specs = [pl.BlockSpec((1, 2, 2, RET_DH), lambda b, p: (p, 0, 0, 0)),
                blk(COL_RT_Q), blk(COL_RT_K), blk(COL_RT_V), blk(COL_RT_G),
                pl.BlockSpec((1, LANES), lambda b, p: (0, p))]
    args = [theta, z, z, z, z, ret_norm.reshape(1, RET_HEADS * RET_DH)]
    y_spec = pl.BlockSpec((rows, LANES), lambda b, p: (b, p))
    y_shape = jax.ShapeDtypeStruct((n_batch * t_len, BRANCH_W), F32)
    if chained:
        sblk = pl.BlockSpec((1, 2, RET_DH, RET_DH), lambda b, p: (b, p, 0, 0))
        in_specs += [sblk, sblk]
        args += list(state)
        out_specs, out_shape = y_spec, y_shape
    else:
        assert t_len == RET_BLOCK
        sblk = pl.BlockSpec((n_chunks, 2, RET_DH, RET_DH), lambda b, p: (b, p, 0, 0))
        s_shape = jax.ShapeDtypeStruct((n_batch, RET_HEADS, RET_DH, RET_DH), F32)
        out_specs, out_shape = [y_spec, sblk, sblk], [y_shape, s_shape, s_shape]
    return pl.pallas_call(
        functools.partial(_ret_kernel, n_chunks=n_chunks, chained=chained),
        grid=(n_steps, RET_HEADS // 2),
        in_specs=in_specs, out_specs=out_specs, out_shape=out_shape,
        compiler_params=_params("parallel", "parallel"),
        name="retention_lat" if chained else "retention_ctx",
    )(*args)


def _shift_rows(x, k, fill, up):
    t_len = x.shape[0]
    if k % SUBLANES == 0:
        pad = jnp.full((k, x.shape[1]), fill, x.dtype)
        return jnp.concatenate([x[k:], pad], axis=0) if up else jnp.concatenate([pad, x[:t_len - k]], axis=0)
    row = lax.broadcasted_iota(jnp.int32, x.shape, 0)
    if up:
        return jnp.where(row < t_len - k, pltpu.roll(x, t_len - k, axis=0), fill)
    return jnp.where(row >= k, pltpu.roll(x, k, axis=0), fill)


def _linear_scan(a, u, up):
    t_len = a.shape[0]
    k = 1
    while k < t_len:
        u = a * _shift_rows(u, k, 0.0, up) + u
        if 2 * k < t_len:
            a = a * _shift_rows(a, k, 1.0, up)
        k *= 2
    return u


def _lru_kernel(*refs, has_state, write_state):
    x_ref, g_ref, vec_ref, w_ref = refs[:4]
    refs = refs[4:]
    if has_state:
        h0f_ref, h0b_ref = refs[:2]
        refs = refs[2:]
    y_ref = refs[0]
    if write_state:
        hf_ref, hb_ref = refs[1:3]
    x = x_ref[...]
    t_len = x.shape[0]
    row = lax.broadcasted_iota(jnp.int32, x.shape, 0)
    vec = lambda i: vec_ref[i:i + 1, :]
    xd = (vec(0) * _shift_rows(x, 1, 0.0, False) + vec(1) * x + vec(2) * _shift_rows(x, 1, 0.0, True)
          + vec(3) * _shift_rows(x, 2, 0.0, True) + vec(4))

    def gates(wa, wx, ba, bx, lam):
        r = _sigmoid(_dot(xd, wa) + ba)
        i = _sigmoid(_dot(xd, wx) + bx)
        nl = -lam
        softplus = jnp.maximum(nl, 0.0) + jnp.log1p(jnp.exp(-jnp.abs(nl)))
        log_a = -LRU_C * r * softplus
        a = jnp.exp(log_a)
        return a, jnp.sqrt(-jnp.tanh(log_a) * (a * a + 1.0)) * (i * xd)

    a, u = gates(w_ref[0, 0], w_ref[1, 0], vec(5), vec(6), vec(7))
    if has_state:
        u = u + jnp.where(row == 0, a * h0f_ref[0], 0.0)
    h_f = _linear_scan(a, u, False)
    a, u = gates(w_ref[2, 0], w_ref[3, 0], vec(8), vec(9), vec(10))
    if has_state:
        u = u + jnp.where(row == t_len - 1, a * h0b_ref[0], 0.0)
    h_b = _linear_scan(a, u, True)
    g = g_ref[...]
    gelu = 0.5 * g * (1.0 + jnp.tanh(math.sqrt(2.0 / math.pi) * (g + 0.044715 * (g * g * g))))
    y_ref[...] = (h_f + h_b) * gelu
    if write_state:
        hf_ref[0] = h_f[t_len - 1:t_len, :]
        hb_ref[0] = h_b[0:1, :]


def _rglru(z, vecs, w_gate, n_batch, t_len, state=None):
    blk = lambda c0: pl.BlockSpec((t_len, LANES), lambda b, j: (b, c0 // LANES + j))
    hblk = pl.BlockSpec((1, 1, LANES), lambda b, j: (b, 0, j))
    in_specs = [blk(COL_LR_X), blk(COL_LR_G),
                pl.BlockSpec((11, LANES), lambda b, j: (0, j)),
                pl.BlockSpec((4, 1, LANES, LANES), lambda b, j: (0, j, 0, 0))]
    args = [z, z, vecs, w_gate]
    y_spec = pl.BlockSpec((t_len, LANES), lambda b, j: (b, j))
    y_shape = jax.ShapeDtypeStruct((n_batch * t_len, BRANCH_W), F32)
    if state is not None:
        in_specs += [hblk, hblk]
        args += [s.reshape(n_batch, 1, LRU_WIDTH) for s in state]
        out_specs, out_shape = y_spec, y_shape
    else:
        h_shape = jax.ShapeDtypeStruct((n_batch, 1, LRU_WIDTH), F32)
        out_specs, out_shape = [y_spec, hblk, hblk], [y_shape, h_shape, h_shape]
    return pl.pallas_call(
        functools.partial(_lru_kernel, has_state=state is not None, write_state=state is None),
        grid=(n_batch, LRU_WIDTH // LANES),
        in_specs=in_specs, out_specs=out_specs, out_shape=out_shape,
        compiler_params=_params("parallel", "parallel"),
        name="rglru_lat" if state is not None else "rglru_ctx",
    )(*args)


def _pair_block_diag(w):
    bw = LRU_WIDTH // LRU_BLOCKS
    w = w.reshape(LRU_BLOCKS // 2, 2, bw, bw)
    zero = jnp.zeros_like(w[:, 0])
    top = jnp.concatenate([w[:, 0], zero], axis=2)
    bot = jnp.concatenate([zero, w[:, 1]], axis=2)
    return jnp.concatenate([top, bot], axis=1)


def _merge_kernel(x_ref, ya_ref, yb_ref, yc_ref, yd_ref, g0_ref, g1_ref, g2_ref, g3_ref,
                  wb_ref, wo_ref, n_ref, m_ref, o_ref):
    acc = None
    for k, (y_ref, g_ref) in enumerate(((ya_ref, g0_ref), (yb_ref, g1_ref), (yc_ref, g2_ref), (yd_ref, g3_ref))):
        term = _sigmoid(g_ref[...]) * jnp.dot(y_ref[...].astype(BF16), wb_ref[k], preferred_element_type=F32)
        acc = term if acc is None else acc + term
    out = jnp.dot(acc.astype(BF16), wo_ref[...], preferred_element_type=F32)
    o_ref[...] = x_ref[...] + m_ref[0, 2:3, :] * _rms(out, n_ref[...])


def _merge(x, ys, z, mod, wb_bf, wo_bf, gain, mod_row):
    n_tok = x.shape[0]
    tm = 256
    yblk = pl.BlockSpec((tm, BRANCH_W), lambda i: (i, 0))
    gblk = lambda k: pl.BlockSpec((tm, D_MODEL), lambda i: (i, COL_GATE // D_MODEL + k))
    return pl.pallas_call(
        _merge_kernel,
        grid=(n_tok // tm,),
        in_specs=[pl.BlockSpec((tm, D_MODEL), lambda i: (i, 0)), yblk, yblk, yblk, yblk,
                  gblk(0), gblk(1), gblk(2), gblk(3),
                  pl.BlockSpec((N_BRANCH, BRANCH_W, D_MODEL), lambda i: (0, 0, 0)),
                  pl.BlockSpec((D_MODEL, D_MODEL), lambda i: (0, 0)),
                  pl.BlockSpec((1, D_MODEL), lambda i: (0, 0)),
                  pl.BlockSpec((1, 6, D_MODEL), lambda i: (mod_row(i, tm), 0, 0))],
        out_specs=pl.BlockSpec((tm, D_MODEL), lambda i: (i, 0)),
        out_shape=jax.ShapeDtypeStruct((n_tok, D_MODEL), F32),
        compiler_params=_params("parallel"),
        name="merge",
    )(x, *ys, z, z, z, z, wb_bf, wo_bf, gain.reshape(1, D_MODEL), mod)


def _mlp_kernel(x_ref, m_ref, gpre_ref, gpost_ref, w1_ref, w2_ref, o_ref):
    x = x_ref[...]
    h = (_rms(x, gpre_ref[...]) * (1.0 + m_ref[0, 4:5, :]) + m_ref[0, 3:4, :]).astype(BF16)
    ff_chunk = D_MODEL
    y = None
    for j in range(D_FF // ff_chunk):
        sl = slice(j * ff_chunk, (j + 1) * ff_chunk)
        a = jnp.maximum(jnp.dot(h, w1_ref[:, sl], preferred_element_type=F32), 0.0)
        part = jnp.dot((a * a).astype(BF16), w2_ref[sl, :], preferred_element_type=F32)
        y = part if y is None else y + part
    o_ref[...] = x + m_ref[0, 5:6, :] * _rms(y, gpost_ref[...])


def _mlp(x, mod, gpre, gpost, w1_bf, w2_bf, mod_row):
    n_tok = x.shape[0]
    tm = 256
    return pl.pallas_call(
        _mlp_kernel,
        grid=(n_tok // tm,),
        in_specs=[pl.BlockSpec((tm, D_MODEL), lambda i: (i, 0)),
                  pl.BlockSpec((1, 6, D_MODEL), lambda i: (mod_row(i, tm), 0, 0)),
                  pl.BlockSpec((1, D_MODEL), lambda i: (0, 0)),
                  pl.BlockSpec((1, D_MODEL), lambda i: (0, 0)),
                  pl.BlockSpec((D_MODEL, D_FF), lambda i: (0, 0)),
                  pl.BlockSpec((D_FF, D_MODEL), lambda i: (0, 0))],
        out_specs=pl.BlockSpec((tm, D_MODEL), lambda i: (i, 0)),
        out_shape=jax.ShapeDtypeStruct((n_tok, D_MODEL), F32),
        compiler_params=_params("parallel"),
        name="mlp",
    )(x, mod, gpre.reshape(1, D_MODEL), gpost.reshape(1, D_MODEL), w1_bf, w2_bf)


def _ctx_mod_row(i, tm):
    return 0


def _lat_mod_row(i, tm):
    return 1 + i // (DEC_SEQ // tm)


def kernel(x_prompt, x_sample, cache_na_k, cache_na_v, cache_diff_k, cache_diff_v, state_ret_fwd, state_ret_bwd, state_lru_fwd, state_lru_bwd, c, c_ctx, ada_w, ada_b, norm_mix_pre, norm_mix_post, norm_ffn_pre, norm_ffn_post, w_in, na_rpb, diff_lq1, diff_lk1, diff_lq2, diff_lk2, diff_norm, ret_theta_fwd, ret_theta_bwd, ret_norm, lru_conv_w, lru_conv_b, lru_wa_fwd, lru_ba_fwd, lru_wx_fwd, lru_bx_fwd, lru_lam_fwd, lru_wa_bwd, lru_ba_bwd, lru_wx_bwd, lru_bx_bwd, lru_lam_bwd, w_branch, w_out, mlp_w1, mlp_w2):
    xc = x_prompt.reshape(BATCH * SEQ, D_MODEL)
    xl = x_sample.reshape(DEC_BATCH * DEC_SEQ, D_MODEL)
    cond = jnp.concatenate([c_ctx[None, :], c, jnp.zeros((COND_PAD - N_COND, D_MODEL), F32)], axis=0)
    mods = _modulation(cond.T, ada_w, ada_b).reshape(DEPTH, COND_PAD, 6, D_MODEL)
    rope_tables = _rope_tables()
    col_tables = _na_col_tables(na_rpb)
    outs = [[] for _ in range(8)]
    for l in range(DEPTH):
        mod = mods[l]
        lam_init = 0.8 - 0.6 * math.exp(-0.3 * l)
        w_in_bf = w_in[l].astype(BF16)
        wb_bf = w_branch[l].astype(BF16)
        wo_bf = w_out[l].astype(BF16)
        w1_bf = mlp_w1[l].astype(BF16)
        w2_bf = mlp_w2[l].astype(BF16)
        lam_p = jnp.stack([diff_lq1[l], diff_lk1[l], diff_lq2[l], diff_lk2[l]])
        theta = jnp.broadcast_to(jnp.stack([ret_theta_fwd[l], ret_theta_bwd[l]], axis=1)[:, :, None],
                                 (RET_HEADS, 2, RET_DH)).reshape(RET_HEADS // 2, 2, 2, RET_DH)
        vecs = jnp.concatenate([lru_conv_w[l], lru_conv_b[l][None],
                                lru_ba_fwd[l][None], lru_bx_fwd[l][None], lru_lam_fwd[l][None],
                                lru_ba_bwd[l][None], lru_bx_bwd[l][None], lru_lam_bwd[l][None]], axis=0)
        w_gate = jnp.stack([_pair_block_diag(w) for w in
                            (lru_wa_fwd[l], lru_wx_fwd[l], lru_wa_bwd[l], lru_wx_bwd[l])])
        bias = _na_bias_blocks(col_tables[l])

        zc = _inproj(xc, mod, norm_mix_pre[l], w_in_bf, _ctx_mod_row)
        ya = _ctx_na(zc)
        yb = _ctx_diff(zc, lam_p, diff_norm[l], lam_init)
        yc, s_rf, s_rb = _retention(zc, theta, ret_norm[l], BATCH, SEQ)
        yd, h_lf, h_lb = _rglru(zc, vecs, w_gate, BATCH, SEQ)
        xc = _merge(xc, (ya, yb, yc, yd), zc, mod, wb_bf, wo_bf, norm_mix_post[l], _ctx_mod_row)
        xc = _mlp(xc, mod, norm_ffn_pre[l], norm_ffn_post[l], w1_bf, w2_bf, _ctx_mod_row)
        z4 = zc.reshape(BATCH, SEQ, IN_WIDTH)
        heads = lambda c0, n, d: z4[:, :, c0:c0 + n * d].reshape(BATCH, SEQ, n, d).transpose(0, 2, 1, 3)
        outs[0].append(heads(COL_NA_K, NA_HEADS, NA_DH))
        outs[1].append(heads(COL_NA_V, NA_HEADS, NA_DH))
        outs[2].append(heads(COL_DF_K, 2 * DIFF_HEADS, DIFF_DH).reshape(BATCH, 2, DIFF_HEADS, SEQ, DIFF_DH))
        outs[3].append(heads(COL_DF_V, DIFF_HEADS, 2 * DIFF_DH))
        outs[4].append(s_rf)
        outs[5].append(s_rb)
        outs[6].append(h_lf.reshape(BATCH, LRU_WIDTH))
        outs[7].append(h_lb.reshape(BATCH, LRU_WIDTH))

        zl = _inproj(xl, mod, norm_mix_pre[l], w_in_bf, _lat_mod_row)
        ya = _lat_na(zl, cache_na_k, cache_na_v, bias, l)
        yb = _lat_diff(zl, cache_diff_k, cache_diff_v, lam_p, diff_norm[l], rope_tables, lam_init, l)
        yc = _retention(zl, theta, ret_norm[l], DEC_BATCH, DEC_SEQ,
                        state=(state_ret_fwd[:, l], state_ret_bwd[:, l]))
        yd = _rglru(zl, vecs, w_gate, DEC_BATCH, DEC_SEQ, state=(state_lru_fwd[:, l], state_lru_bwd[:, l]))
        xl = _merge(xl, (ya, yb, yc, yd), zl, mod, wb_bf, wo_bf, norm_mix_post[l], _lat_mod_row)
        xl = _mlp(xl, mod, norm_ffn_pre[l], norm_ffn_post[l], w1_bf, w2_bf, _lat_mod_row)

    stacked = [jnp.stack(o, axis=1) for o in outs]
    return (xc.reshape(BATCH, SEQ, D_MODEL), xl.reshape(DEC_BATCH, DEC_SEQ, D_MODEL), *stacked)
```

```python
import functools
import math

import numpy as np
import jax
import jax.numpy as jnp
from jax import lax
from jax.experimental import pallas as pl
from jax.experimental.pallas import tpu as pltpu

F32 = jnp.float32
BF16 = jnp.bfloat16

D_MODEL = 1024
BATCH = 32
SEQ = 256
DEPTH = 4
DEC_BATCH = 4
DEC_SEQ = 2048
PAST_LEN = 512
GRID_W = 64
NA_HEADS = 8
NA_DH = 64
NA_ROWS = 8
NA_COLS = 16
DIFF_HEADS = 4
DIFF_DH = 64
RET_HEADS = 8
RET_DH = 64
LRU_WIDTH = 512
LRU_BLOCKS = 8
LRU_CONV = 4
LRU_C = 8.0
N_BRANCH = 4
BRANCH_W = 512
D_FF = 4 * D_MODEL
ROPE_BASE = 10000.0
EPS = 1e-6
NEG_INF = -1e30

COL_NA_Q, COL_NA_K, COL_NA_V = 0, 512, 1024
COL_DF_Q, COL_DF_K, COL_DF_V = 1536, 2048, 2560
COL_RT_Q, COL_RT_K, COL_RT_V, COL_RT_G = 3072, 3584, 4096, 4608
COL_LR_X, COL_LR_G = 5120, 5632
COL_GATE = 6144
Z_WIDTH = COL_GATE

N_COND = 1 + DEC_BATCH
COND_PAD = 8
LANES = 128
SUBLANES = 8
VMEM_LIMIT = 56 * 1024 * 1024


def _params(*sem):
    return pltpu.CompilerParams(dimension_semantics=sem, vmem_limit_bytes=VMEM_LIMIT)


def _dot(a, b):
    return jnp.dot(a.astype(BF16), b.astype(BF16), preferred_element_type=F32)


def _dot_nt(a, b):
    return lax.dot_general(a.astype(BF16), b.astype(BF16), (((1,), (1,)), ((), ())),
                           preferred_element_type=F32)


def _dot_tn(a, b):
    return lax.dot_general(a.astype(BF16), b.astype(BF16), (((0,), (0,)), ((), ())),
                           preferred_element_type=F32)


def _rms(x, g):
    return x * lax.rsqrt(jnp.mean(x * x, axis=-1, keepdims=True) + EPS) * g


def _sigmoid(x):
    return 1.0 / (1.0 + jnp.exp(-x))


LOG2E = math.log2(math.e)


def _exp2_weights(parts):
    m = functools.reduce(jnp.maximum, [jnp.max(s, axis=-1, keepdims=True) for s in parts])
    return [jnp.exp2(s - m).astype(BF16) for s in parts]


def _with_ones(v):
    return jnp.concatenate([v.astype(BF16), jnp.ones(v.shape, BF16)], axis=1)


def _normalised(r):
    w = r.shape[1] // 2
    return r[:, :w] / r[:, w:]


def _mod_kernel(ct_ref, w_ref, b_ref, o_ref):
    ct = ct_ref[...]
    s = ct * _sigmoid(ct)
    w = w_ref[0]
    rows = [jnp.sum(s[:, r:r + 1] * w, axis=0, keepdims=True) + b_ref[0] for r in range(N_COND)]
    rows.append(jnp.zeros((COND_PAD - N_COND, w.shape[1]), F32))
    o_ref[0] = jnp.concatenate(rows, axis=0)


def _modulation(cond_t, ada_w, ada_b):
    tn = 768
    n = 6 * D_MODEL
    return pl.pallas_call(
        _mod_kernel,
        grid=(DEPTH, n // tn),
        in_specs=[pl.BlockSpec((D_MODEL, COND_PAD), lambda l, j: (0, 0)),
                  pl.BlockSpec((1, D_MODEL, tn), lambda l, j: (l, 0, j)),
                  pl.BlockSpec((1, 1, tn), lambda l, j: (l, 0, j))],
        out_specs=pl.BlockSpec((1, COND_PAD, tn), lambda l, j: (l, 0, j)),
        out_shape=jax.ShapeDtypeStruct((DEPTH, COND_PAD, n), F32),
        compiler_params=_params("parallel", "parallel"),
        name="modulation",
    )(cond_t, ada_w, ada_b.reshape(DEPTH, 1, n))


def _inproj_kernel(x_ref, m_ref, g_ref, w_ref, z_ref, h_scr):
    @pl.when(pl.program_id(1) == 0)
    def _():
        h = _rms(x_ref[...], g_ref[...]) * (1.0 + m_ref[0, 1:2, :]) + m_ref[0, 0:1, :]
        h_scr[...] = h.astype(BF16)

    z_ref[...] = jnp.dot(h_scr[...], w_ref[...], preferred_element_type=F32)


def _inproj(x, mod, gain, w_bf, mod_row):
    n_tok = x.shape[0]
    tm, tn = 2048, 1024
    return pl.pallas_call(
        _inproj_kernel,
        grid=(n_tok // tm, Z_WIDTH // tn),
        in_specs=[pl.BlockSpec((tm, D_MODEL), lambda i, j: (i, 0)),
                  pl.BlockSpec((1, 6, D_MODEL), lambda i, j: (mod_row(i, tm), 0, 0)),
                  pl.BlockSpec((1, D_MODEL), lambda i, j: (0, 0)),
                  pl.BlockSpec((D_MODEL, tn), lambda i, j: (0, j))],
        out_specs=pl.BlockSpec((tm, tn), lambda i, j: (i, j)),
        out_shape=jax.ShapeDtypeStruct((n_tok, Z_WIDTH), F32),
        scratch_shapes=[pltpu.VMEM((tm, D_MODEL), BF16)],
        compiler_params=_params("parallel", "arbitrary"),
        name="inproj",
    )(x, mod, gain.reshape(1, D_MODEL), w_bf)


def _ctx_na_kernel(q_ref, k_ref, v_ref, o_ref):
    qs = NA_DH ** -0.5 * LOG2E
    for h in range(NA_HEADS):
        sl = slice(h * NA_DH, (h + 1) * NA_DH)
        (e,) = _exp2_weights([_dot_nt(q_ref[:, sl] * qs, k_ref[:, sl])])
        o_ref[:, sl] = _normalised(jnp.dot(e, _with_ones(v_ref[:, sl]), preferred_element_type=F32))


def _ctx_na(z):
    blk = lambda j: pl.BlockSpec((SEQ, 512), lambda b: (b, j))
    return pl.pallas_call(
        _ctx_na_kernel,
        grid=(BATCH,),
        in_specs=[blk(COL_NA_Q // 512), blk(COL_NA_K // 512), blk(COL_NA_V // 512)],
        out_specs=pl.BlockSpec((SEQ, 512), lambda b: (b, 0)),
        out_shape=jax.ShapeDtypeStruct((BATCH * SEQ, BRANCH_W), F32),
        compiler_params=_params("parallel"),
        name="ctx_na",
    )(z, z, z)


def _diff_lambda(lam_ref, lam_init):
    a = jnp.sum(lam_ref[0:1, :] * lam_ref[1:2, :], axis=-1, keepdims=True)
    b = jnp.sum(lam_ref[2:3, :] * lam_ref[3:4, :], axis=-1, keepdims=True)
    return jnp.exp(a) - jnp.exp(b) + lam_init


def _ctx_diff_kernel(lam_ref, q_ref, k_ref, v_ref, dn_ref, o_ref, *, lam_init):
    qs = DIFF_DH ** -0.5 * LOG2E
    lam = _diff_lambda(lam_ref, lam_init)
    w = 2 * DIFF_DH
    for h in range(DIFF_HEADS):
        vx = _with_ones(v_ref[:, h * w:(h + 1) * w])
        att = []
        for c in range(2):
            c0 = c * DIFF_HEADS * DIFF_DH + h * DIFF_DH
            sl = slice(c0, c0 + DIFF_DH)
            (e,) = _exp2_weights([_dot_nt(q_ref[:, sl] * qs, k_ref[:, sl])])
            att.append(_normalised(jnp.dot(e, vx, preferred_element_type=F32)))
        o = att[0] - lam * att[1]
        o_ref[:, h * w:(h + 1) * w] = _rms(o, dn_ref[...]) * (1.0 - lam_init)


def _ctx_diff(z, lam_p, diff_norm, lam_init):
    blk = lambda j: pl.BlockSpec((SEQ, 512), lambda b: (b, j))
    return pl.pallas_call(
        functools.partial(_ctx_diff_kernel, lam_init=lam_init),
        grid=(BATCH,),
        in_specs=[pl.BlockSpec((4, DIFF_DH), lambda b: (0, 0)),
                  blk(COL_DF_Q // 512), blk(COL_DF_K // 512), blk(COL_DF_V // 512),
                  pl.BlockSpec((1, 2 * DIFF_DH), lambda b: (0, 0))],
        out_specs=pl.BlockSpec((SEQ, 512), lambda b: (b, 0)),
        out_shape=jax.ShapeDtypeStruct((BATCH * SEQ, BRANCH_W), F32),
        compiler_params=_params("parallel"),
        name="ctx_diff",
    )(lam_p, z, z, z, diff_norm.reshape(1, 2 * DIFF_DH))


NA_GRID_ROWS = DEC_SEQ // GRID_W
NA_KEY_ROWS = min(NA_ROWS, NA_GRID_ROWS)


NA_DR = 2 * NA_ROWS - 1
NA_DC = 2 * NA_COLS - 1


def _na_col_kernel(rpb_ref, idx_ref, o_ref):
    idx = jnp.broadcast_to(idx_ref[...], o_ref.shape)
    acc = jnp.full(o_ref.shape, NEG_INF, F32)
    for d in range(NA_DC):
        acc = jnp.where(idx == d, jnp.broadcast_to(rpb_ref[:, d:d + 1], o_ref.shape), acc)
    o_ref[...] = acc * LOG2E


def _na_col_tables(na_rpb):
    qc = np.arange(GRID_W)[:, None]
    kc = np.arange(GRID_W)[None, :]
    cstart = np.clip(qc - NA_COLS // 2, 0, GRID_W - NA_COLS)
    ok = (kc >= cstart) & (kc < cstart + NA_COLS)
    idx = np.where(ok, kc - qc + NA_COLS - 1, -1).astype(np.int32).reshape(1, GRID_W * GRID_W)
    rows = NA_HEADS * NA_DR
    rpb = jnp.pad(na_rpb.reshape(DEPTH * rows, NA_DC), ((0, 0), (0, 32 - NA_DC)))
    t = pl.pallas_call(
        _na_col_kernel,
        grid=(DEPTH,),
        in_specs=[pl.BlockSpec((rows, 32), lambda l: (l, 0)),
                  pl.BlockSpec((1, GRID_W * GRID_W), lambda l: (0, 0))],
        out_specs=pl.BlockSpec((rows, GRID_W * GRID_W), lambda l: (l, 0)),
        out_shape=jax.ShapeDtypeStruct((DEPTH * rows, GRID_W * GRID_W), F32),
        compiler_params=_params("parallel"),
        name="na_col_tables",
    )(rpb, jnp.asarray(idx))
    return t.reshape(DEPTH, NA_HEADS, NA_DR, GRID_W, GRID_W)


NA_QROWS = 4
NA_UROWS = 12
NA_QBLK = NA_QROWS * GRID_W
NA_UKEYS = NA_UROWS * GRID_W
NA_NBLK = NA_GRID_ROWS // NA_QROWS


def _na_union_start(r0):
    return int(np.clip(r0 - NA_KEY_ROWS // 2, 0, NA_GRID_ROWS - NA_UROWS))


def _na_bias_blocks(t):
    neg = jnp.full((NA_HEADS, GRID_W, GRID_W), NEG_INF, F32)
    variants = []
    for r0 in (0, NA_QROWS, NA_GRID_ROWS - NA_QROWS):
        us = _na_union_start(r0)
        q_rows = []
        for j in range(NA_QROWS):
            r = r0 + j
            rs = int(np.clip(r - NA_KEY_ROWS // 2, 0, NA_GRID_ROWS - NA_KEY_ROWS))
            blocks = [t[:, us + i - r + NA_ROWS - 1] if rs <= us + i < rs + NA_KEY_ROWS else neg
                      for i in range(NA_UROWS)]
            q_rows.append(jnp.concatenate(blocks, axis=-1))
        variants.append(jnp.concatenate(q_rows, axis=-2))
    return jnp.stack(variants)


def _lat_na_kernel(q_ref, k_ref, v_ref, kc_ref, vc_ref, bias_ref, o_ref, k_scr, vx_scr, kc_scr, vcx_scr):
    qs = NA_DH ** -0.5 * LOG2E
    for hh in range(2):
        sl = slice(hh * NA_DH, (hh + 1) * NA_DH)
        k_scr[hh] = k_ref[:, sl].astype(BF16)
        vx_scr[hh] = _with_ones(v_ref[:, sl])
        kc_scr[hh] = kc_ref[0, 0, hh].astype(BF16)
        vcx_scr[hh] = _with_ones(vc_ref[0, 0, hh])

    def body(blk, carry):
        us = jnp.clip(blk * NA_QROWS - NA_KEY_ROWS // 2, 0, NA_GRID_ROWS - NA_UROWS)
        variant = jnp.where(blk == 0, 0, jnp.where(blk == NA_NBLK - 1, 2, 1))
        q0 = pl.multiple_of(blk * NA_QBLK, NA_QBLK)
        keys = pl.ds(pl.multiple_of(us * GRID_W, GRID_W), NA_UKEYS)
        outs = []
        for hh in range(2):
            q = (q_ref[pl.ds(q0, NA_QBLK), hh * NA_DH:(hh + 1) * NA_DH] * qs).astype(BF16)
            s_loc = _dot_nt(q, k_scr[hh, keys]) + bias_ref[variant, hh]
            s_ctx = _dot_nt(q, kc_scr[hh])
            e_loc, e_ctx = _exp2_weights([s_loc, s_ctx])
            outs.append(_normalised(jnp.dot(e_loc, vx_scr[hh, keys], preferred_element_type=F32)
                                    + jnp.dot(e_ctx, vcx_scr[hh], preferred_element_type=F32)))
        o_ref[pl.ds(q0, NA_QBLK), :] = jnp.concatenate(outs, axis=1)
        return carry

    lax.fori_loop(0, NA_NBLK, body, 0)


def _lat_na(z, cache_k, cache_v, bias, l):
    blk = lambda c0: pl.BlockSpec((DEC_SEQ, LANES), lambda p, b: (b, c0 // LANES + p))
    cblk = pl.BlockSpec((1, 1, 2, PAST_LEN, NA_DH), lambda p, b: (b, l, p, 0, 0))
    return pl.pallas_call(
        _lat_na_kernel,
        grid=(NA_HEADS // 2, DEC_BATCH),
        in_specs=[blk(COL_NA_Q), blk(COL_NA_K), blk(COL_NA_V), cblk, cblk,
                  pl.BlockSpec((3, 2, NA_QBLK, NA_UKEYS), lambda p, b: (0, p, 0, 0))],
        out_specs=pl.BlockSpec((DEC_SEQ, LANES), lambda p, b: (b, p)),
        out_shape=jax.ShapeDtypeStruct((DEC_BATCH * DEC_SEQ, BRANCH_W), F32),
        scratch_shapes=[pltpu.VMEM((2, DEC_SEQ, NA_DH), BF16), pltpu.VMEM((2, DEC_SEQ, 2 * NA_DH), BF16),
                        pltpu.VMEM((2, PAST_LEN, NA_DH), BF16), pltpu.VMEM((2, PAST_LEN, 2 * NA_DH), BF16)],
        compiler_params=_params("parallel", "parallel"),
        name="lat_na",
    )(z, z, z, cache_k, cache_v, bias)


def _rope_tables():
    t = jnp.arange(DEC_SEQ)
    quarter = DIFF_DH // 4
    inv = ROPE_BASE ** (-jnp.arange(quarter, dtype=F32) / quarter)
    ang_r = (t // GRID_W).astype(F32)[:, None] * inv
    ang_c = (t % GRID_W).astype(F32)[:, None] * inv
    lane = np.arange(LANES) % DIFF_DH
    idx = lane % quarter
    use_r = jnp.asarray(lane < DIFF_DH // 2)[None, :]
    first = jnp.asarray((lane % (2 * quarter)) < quarter)[None, :]
    ang = jnp.where(use_r, ang_r[:, idx], ang_c[:, idx])
    cos = jnp.cos(ang)
    sin = jnp.sin(ang)
    return cos, jnp.where(first, -sin, 0.0), jnp.where(first, 0.0, sin)


def _rope(x, cos, sin_a, sin_b):
    quarter = DIFF_DH // 4
    up = pltpu.roll(x, LANES - quarter, axis=1)
    down = pltpu.roll(x, quarter, axis=1)
    return x * cos + up * sin_a + down * sin_b


def _lat_diff_kernel(lam_ref, q_ref, k_ref, v_ref, kc_ref, vc_ref, cos_ref, sa_ref, sb_ref, dn_ref, o_ref,
                     k_scr, vx_scr, *, lam_init, tq):
    qs = DIFF_DH ** -0.5 * LOG2E
    n_slab = 512 // LANES
    w = 2 * DIFF_DH
    lat = slice(0, DEC_SEQ)
    ctx = slice(DEC_SEQ, DEC_SEQ + PAST_LEN)

    @pl.when(pl.program_id(1) == 0)
    def _():
        for j in range(n_slab):
            krot = _rope(k_ref[:, j * LANES:(j + 1) * LANES], cos_ref[...], sa_ref[...], sb_ref[...]).astype(BF16)
            for half in range(LANES // DIFF_DH):
                k_scr[j * (LANES // DIFF_DH) + half, lat, :] = krot[:, half * DIFF_DH:(half + 1) * DIFF_DH]
        for c in range(2):
            for h in range(DIFF_HEADS):
                k_scr[c * DIFF_HEADS + h, ctx, :] = kc_ref[0, 0, c, h].astype(BF16)
        for h in range(DIFF_HEADS):
            vx_scr[h, lat, :] = _with_ones(v_ref[:, h * w:(h + 1) * w])
            vx_scr[h, ctx, :] = _with_ones(vc_ref[0, 0, h])

    r0 = pl.multiple_of(pl.program_id(1) * tq, tq)
    cos, sa, sb = cos_ref[pl.ds(r0, tq), :], sa_ref[pl.ds(r0, tq), :], sb_ref[pl.ds(r0, tq), :]
    qrot = jnp.concatenate([_rope(q_ref[:, j * LANES:(j + 1) * LANES], cos, sa, sb) * qs for j in range(n_slab)],
                           axis=1).astype(BF16)
    lam = _diff_lambda(lam_ref, lam_init)
    for h in range(DIFF_HEADS):
        att = []
        for c in range(2):
            c0 = c * DIFF_HEADS * DIFF_DH + h * DIFF_DH
            (e,) = _exp2_weights([_dot_nt(qrot[:, c0:c0 + DIFF_DH], k_scr[c * DIFF_HEADS + h])])
            att.append(_normalised(jnp.dot(e, vx_scr[h], preferred_element_type=F32)))
        o = att[0] - lam * att[1]
        o_ref[:, h * w:(h + 1) * w] = _rms(o, dn_ref[...]) * (1.0 - lam_init)


def _lat_diff(z, cache_k, cache_v, lam_p, diff_norm, tables, lam_init, l):
    tq = 256
    nq = DEC_SEQ // tq
    tab = pl.BlockSpec((DEC_SEQ, LANES), lambda b, i: (0, 0))
    return pl.pallas_call(
        functools.partial(_lat_diff_kernel, lam_init=lam_init, tq=tq),
        grid=(DEC_BATCH, nq),
        in_specs=[pl.BlockSpec((4, DIFF_DH), lambda b, i: (0, 0)),
                  pl.BlockSpec((tq, 512), lambda b, i: (b * nq + i, COL_DF_Q // 512)),
                  pl.BlockSpec((DEC_SEQ, 512), lambda b, i: (b, COL_DF_K // 512)),
                  pl.BlockSpec((DEC_SEQ, 512), lambda b, i: (b, COL_DF_V // 512)),
                  pl.BlockSpec((1, 1, 2, DIFF_HEADS, PAST_LEN, DIFF_DH), lambda b, i: (b, l, 0, 0, 0, 0)),
                  pl.BlockSpec((1, 1, DIFF_HEADS, PAST_LEN, 2 * DIFF_DH), lambda b, i: (b, l, 0, 0, 0)),
                  tab, tab, tab,
                  pl.BlockSpec((1, 2 * DIFF_DH), lambda b, i: (0, 0))],
        out_specs=pl.BlockSpec((tq, 512), lambda b, i: (b * nq + i, 0)),
        out_shape=jax.ShapeDtypeStruct((DEC_BATCH * DEC_SEQ, BRANCH_W), F32),
        scratch_shapes=[pltpu.VMEM((2 * DIFF_HEADS, DEC_SEQ + PAST_LEN, DIFF_DH), BF16),
                        pltpu.VMEM((DIFF_HEADS, DEC_SEQ + PAST_LEN, 4 * DIFF_DH), BF16)],
        compiler_params=_params("parallel", "arbitrary"),
        name="lat_diff",
    )(lam_p, z, z, z, cache_k, cache_v, *tables, diff_norm.reshape(1, 2 * DIFF_DH))


RET_BLOCK = 256


def _ret_kernel(*refs, n_chunks, chained):
    th_ref, q_ref, k_ref, v_ref, g_ref, gn_ref = refs[:6]
    if chained:
        s0f_ref, s0b_ref, y_ref = refs[6:]
    else:
        y_ref, sf_ref, sb_ref = refs[6:]
    C = RET_BLOCK
    dh = RET_DH
    rel = (lax.broadcasted_iota(jnp.int32, (C, C), 0) - lax.broadcasted_iota(jnp.int32, (C, C), 1)).astype(F32)
    pos = lax.broadcasted_iota(jnp.int32, (C, dh), 0).astype(F32)
    for hh in range(2):
        sl = slice(hh * dh, (hh + 1) * dh)
        lg = jnp.log1p(-jnp.exp(th_ref[0, hh]))
        lgf, lgb = lg[0:1, :], lg[1:2, :]
        lgf_c = jnp.concatenate([lgf] * (C // dh), axis=1)
        lgb_c = jnp.concatenate([lgb] * (C // dh), axis=1)
        decay = jnp.exp(jnp.where(rel >= 0, rel * lgf_c, -rel * lgb_c))
        xi_f = jnp.exp((pos + 1.0) * lgf)
        zeta_f = jnp.exp((C - 1.0 - pos) * lgf)
        xi_b = jnp.exp((C - pos) * lgb)
        zeta_b = jnp.exp(pos * lgb)
        gn = gn_ref[:, sl]
        rows = [pl.ds(c * C, C) for c in range(n_chunks)]
        ks = [k_ref[r, sl] * (dh ** -0.5) for r in rows]
        vs = [v_ref[r, sl] for r in rows]
        uf = [_dot_tn(k * zeta_f, v) for k, v in zip(ks, vs)]
        ub = [_dot_tn(k * zeta_b, v) for k, v in zip(ks, vs)]
        if chained:
            gc_f = jnp.exp(C * lgf)
            gc_b = jnp.exp(C * lgb)
            s_f = [s0f_ref[0, hh]]
            for c in range(n_chunks - 1):
                s_f.append(gc_f * s_f[c] + uf[c])
            s_b = [s0b_ref[0, hh]]
            for c in range(n_chunks - 1, 0, -1):
                s_b.append(gc_b * s_b[-1] + ub[c])
            s_b = s_b[::-1]
        for c in range(n_chunks):
            q = q_ref[rows[c], sl]
            o = _dot(_dot_nt(q, ks[c]) * decay, vs[c])
            if chained:
                o = o + _dot(q, s_f[c]) * xi_f + _dot(q, s_b[c]) * xi_b
            else:
                sf_ref[c, hh] = uf[c]
                sb_ref[c, hh] = ub[c]
            g = g_ref[rows[c], sl]
            y_ref[rows[c], sl] = _rms(o, gn) * (g * _sigmoid(g))


def _retention(z, theta, ret_norm, n_batch, t_len, state=None):
    chained = state is not None
    n_chunks = t_len // RET_BLOCK if chained else 4
    rows = n_chunks * RET_BLOCK
    n_steps = n_batch * t_len // rows
    blk = lambda c0: pl.BlockSpec((rows, LANES), lambda b, p: (b, c0 // LANES + p))
    in_specs = [pl.BlockSpec((1, 2, 2, RET_DH), lambda b, p: (p, 0, 0, 0)),
                blk(COL_RT_Q), blk(COL_RT_K), blk(COL_RT_V), blk(COL_RT_G),
                pl.BlockSpec((1, LANES), lambda b, p: (0, p))]
    args = [theta, z, z, z, z, ret_norm.reshape(1, RET_HEADS * RET_DH)]
    y_spec = pl.BlockSpec((rows, LANES), lambda b, p: (b, p))
    y_shape = jax.ShapeDtypeStruct((n_batch * t_len, BRANCH_W), F32)
    if chained:
        sblk = pl.BlockSpec((1, 2, RET_DH, RET_DH), lambda b, p: (b, p, 0, 0))
        in_specs += [sblk, sblk]
        args += list(state)
        out_specs, out_shape = y_spec, y_shape
    else:
        assert t_len == RET_BLOCK
        sblk = pl.BlockSpec((n_chunks, 2, RET_DH, RET_DH), lambda b, p: (b, p, 0, 0))
        s_shape = jax.ShapeDtypeStruct((n_batch, RET_HEADS, RET_DH, RET_DH), F32)
        out_specs, out_shape = [y_spec, sblk, sblk], [y_shape, s_shape, s_shape]
    return pl.pallas_call(
        functools.partial(_ret_kernel, n_chunks=n_chunks, chained=chained),
        grid=(n_steps, RET_HEADS // 2),
        in_specs=in_specs, out_specs=out_specs, out_shape=out_shape,
        compiler_params=_params("parallel", "parallel"),
        name="retention_lat" if chained else "retention_ctx",
    )(*args)


def _shift_rows(x, k, fill, up):
    t_len = x.shape[0]
    if k % SUBLANES == 0:
        pad = jnp.full((k, x.shape[1]), fill, x.dtype)
        return jnp.concatenate([x[k:], pad], axis=0) if up else jnp.concatenate([pad, x[:t_len - k]], axis=0)
    row = lax.broadcasted_iota(jnp.int32, x.shape, 0)
    if up:
        return jnp.where(row < t_len - k, pltpu.roll(x, t_len - k, axis=0), fill)
    return jnp.where(row >= k, pltpu.roll(x, k, axis=0), fill)


def _linear_scan(a, u, up):
    t_len = a.shape[0]
    k = 1
    while k < t_len:
        u = a * _shift_rows(u, k, 0.0, up) + u
        if 2 * k < t_len:
            a = a * _shift_rows(a, k, 1.0, up)
        k *= 2
    return u


def _lru_kernel(*refs, has_state, write_state):
    x_ref, g_ref, vec_ref, w_ref = refs[:4]
    refs = refs[4:]
    if has_state:
        h0f_ref, h0b_ref = refs[:2]
        refs = refs[2:]
    y_ref = refs[0]
    if write_state:
        hf_ref, hb_ref = refs[1:3]
    x = x_ref[...]
    t_len = x.shape[0]
    row = lax.broadcasted_iota(jnp.int32, x.shape, 0)
    vec = lambda i: vec_ref[i:i + 1, :]
    xd = (vec(0) * _shift_rows(x, 1, 0.0, False) + vec(1) * x + vec(2) * _shift_rows(x, 1, 0.0, True)
          + vec(3) * _shift_rows(x, 2, 0.0, True) + vec(4))

    def gates(wa, wx, ba, bx, lam):
        r = _sigmoid(_dot(xd, wa) + ba)
        i = _sigmoid(_dot(xd, wx) + bx)
        nl = -lam
        softplus = jnp.maximum(nl, 0.0) + jnp.log1p(jnp.exp(-jnp.abs(nl)))
        log_a = -LRU_C * r * softplus
        a = jnp.exp(log_a)
        return a, jnp.sqrt(-jnp.tanh(log_a) * (a * a + 1.0)) * (i * xd)

    a, u = gates(w_ref[0, 0], w_ref[1, 0], vec(5), vec(6), vec(7))
    if has_state:
        u = u + jnp.where(row == 0, a * h0f_ref[0], 0.0)
    h_f = _linear_scan(a, u, False)
    a, u = gates(w_ref[2, 0], w_ref[3, 0], vec(8), vec(9), vec(10))
    if has_state:
        u = u + jnp.where(row == t_len - 1, a * h0b_ref[0], 0.0)
    h_b = _linear_scan(a, u, True)
    g = g_ref[...]
    gelu = 0.5 * g * (1.0 + jnp.tanh(math.sqrt(2.0 / math.pi) * (g + 0.044715 * (g * g * g))))
    y_ref[...] = (h_f + h_b) * gelu
    if write_state:
        hf_ref[0] = h_f[t_len - 1:t_len, :]
        hb_ref[0] = h_b[0:1, :]


def _rglru(z, vecs, w_gate, n_batch, t_len, state=None):
    blk = lambda c0: pl.BlockSpec((t_len, LANES), lambda b, j: (b, c0 // LANES + j))
    hblk = pl.BlockSpec((1, 1, LANES), lambda b, j: (b, 0, j))
    in_specs = [blk(COL_LR_X), blk(COL_LR_G),
                pl.BlockSpec((11, LANES), lambda b, j: (0, j)),
                pl.BlockSpec((4, 1, LANES, LANES), lambda b, j: (0, j, 0, 0))]
    args = [z, z, vecs, w_gate]
    y_spec = pl.BlockSpec((t_len, LANES), lambda b, j: (b, j))
    y_shape = jax.ShapeDtypeStruct((n_batch * t_len, BRANCH_W), F32)
    if state is not None:
        in_specs += [hblk, hblk]
        args += [s.reshape(n_batch, 1, LRU_WIDTH) for s in state]
        out_specs, out_shape = y_spec, y_shape
    else:
        h_shape = jax.ShapeDtypeStruct((n_batch, 1, LRU_WIDTH), F32)
        out_specs, out_shape = [y_spec, hblk, hblk], [y_shape, h_shape, h_shape]
    return pl.pallas_call(
        functools.partial(_lru_kernel, has_state=state is not None, write_state=state is None),
        grid=(n_batch, LRU_WIDTH // LANES),
        in_specs=in_specs, out_specs=out_specs, out_shape=out_shape,
        compiler_params=_params("parallel", "parallel"),
        name="rglru_lat" if state is not None else "rglru_ctx",
    )(*args)


def _pair_block_diag(w):
    bw = LRU_WIDTH // LRU_BLOCKS
    w = w.reshape(LRU_BLOCKS // 2, 2, bw, bw)
    zero = jnp.zeros_like(w[:, 0])
    top = jnp.concatenate([w[:, 0], zero], axis=2)
    bot = jnp.concatenate([zero, w[:, 1]], axis=2)
    return jnp.concatenate([top, bot], axis=1)


def _merge_kernel(x_ref, ya_ref, yb_ref, yc_ref, yd_ref, wg_ref, wb_ref, wo_ref, gpre_ref, gpost_ref, m_ref, o_ref):
    x = x_ref[...]
    h = (_rms(x, gpre_ref[...]) * (1.0 + m_ref[0, 1:2, :]) + m_ref[0, 0:1, :]).astype(BF16)
    acc = None
    for k, y_ref in enumerate((ya_ref, yb_ref, yc_ref, yd_ref)):
        logits = jnp.dot(h, wg_ref[:, k * D_MODEL:(k + 1) * D_MODEL], preferred_element_type=F32)
        term = _sigmoid(logits) * jnp.dot(y_ref[...].astype(BF16), wb_ref[k], preferred_element_type=F32)
        acc = term if acc is None else acc + term
    out = jnp.dot(acc.astype(BF16), wo_ref[...], preferred_element_type=F32)
    o_ref[...] = x + m_ref[0, 2:3, :] * _rms(out, gpost_ref[...])


def _merge(x, ys, mod, wg_bf, wb_bf, wo_bf, gpre, gpost, mod_row):
    n_tok = x.shape[0]
    tm = 256
    yblk = pl.BlockSpec((tm, BRANCH_W), lambda i: (i, 0))
    return pl.pallas_call(
        _merge_kernel,
        grid=(n_tok // tm,),
        in_specs=[pl.BlockSpec((tm, D_MODEL), lambda i: (i, 0)), yblk, yblk, yblk, yblk,
                  pl.BlockSpec((D_MODEL, N_BRANCH * D_MODEL), lambda i: (0, 0)),
                  pl.BlockSpec((N_BRANCH, BRANCH_W, D_MODEL), lambda i: (0, 0, 0)),
                  pl.BlockSpec((D_MODEL, D_MODEL), lambda i: (0, 0)),
                  pl.BlockSpec((1, D_MODEL), lambda i: (0, 0)),
                  pl.BlockSpec((1, D_MODEL), lambda i: (0, 0)),
                  pl.BlockSpec((1, 6, D_MODEL), lambda i: (mod_row(i, tm), 0, 0))],
        out_specs=pl.BlockSpec((tm, D_MODEL), lambda i: (i, 0)),
        out_shape=jax.ShapeDtypeStruct((n_tok, D_MODEL), F32),
        compiler_params=_params("parallel"),
        name="merge",
    )(x, *ys, wg_bf, wb_bf, wo_bf, gpre.reshape(1, D_MODEL), gpost.reshape(1, D_MODEL), mod)


def _mlp_kernel(x_ref, m_ref, gpre_ref, gpost_ref, w1_ref, w2_ref, o_ref):
    x = x_ref[...]
    h = (_rms(x, gpre_ref[...]) * (1.0 + m_ref[0, 4:5, :]) + m_ref[0, 3:4, :]).astype(BF16)
    ff_chunk = D_MODEL
    y = None
    for j in range(D_FF // ff_chunk):
        sl = slice(j * ff_chunk, (j + 1) * ff_chunk)
        a = jnp.maximum(jnp.dot(h, w1_ref[:, sl], preferred_element_type=F32), 0.0)
        part = jnp.dot((a * a).astype(BF16), w2_ref[sl, :], preferred_element_type=F32)
        y = part if y is None else y + part
    o_ref[...] = x + m_ref[0, 5:6, :] * _rms(y, gpost_ref[...])


def _mlp(x, mod, gpre, gpost, w1_bf, w2_bf, mod_row):
    n_tok = x.shape[0]
    tm = 256
    return pl.pallas_call(
        _mlp_kernel,
        grid=(n_tok // tm,),
        in_specs=[pl.BlockSpec((tm, D_MODEL), lambda i: (i, 0)),
                  pl.BlockSpec((1, 6, D_MODEL), lambda i: (mod_row(i, tm), 0, 0)),
                  pl.BlockSpec((1, D_MODEL), lambda i: (0, 0)),
                  pl.BlockSpec((1, D_MODEL), lambda i: (0, 0)),
                  pl.BlockSpec((D_MODEL, D_FF), lambda i: (0, 0)),
                  pl.BlockSpec((D_FF, D_MODEL), lambda i: (0, 0))],
        out_specs=pl.BlockSpec((tm, D_MODEL), lambda i: (i, 0)),
        out_shape=jax.ShapeDtypeStruct((n_tok, D_MODEL), F32),
        compiler_params=_params("parallel"),
        name="mlp",
    )(x, mod, gpre.reshape(1, D_MODEL), gpost.reshape(1, D_MODEL), w1_bf, w2_bf)


def _ctx_mod_row(i, tm):
    return 0


def _lat_mod_row(i, tm):
    return 1 + i // (DEC_SEQ // tm)


def kernel(x_prompt, x_sample, cache_na_k, cache_na_v, cache_diff_k, cache_diff_v, state_ret_fwd, state_ret_bwd, state_lru_fwd, state_lru_bwd, c, c_ctx, ada_w, ada_b, norm_mix_pre, norm_mix_post, norm_ffn_pre, norm_ffn_post, w_in, na_rpb, diff_lq1, diff_lk1, diff_lq2, diff_lk2, diff_norm, ret_theta_fwd, ret_theta_bwd, ret_norm, lru_conv_w, lru_conv_b, lru_wa_fwd, lru_ba_fwd, lru_wx_fwd, lru_bx_fwd, lru_lam_fwd, lru_wa_bwd, lru_ba_bwd, lru_wx_bwd, lru_bx_bwd, lru_lam_bwd, w_branch, w_out, mlp_w1, mlp_w2):
    xc = x_prompt.reshape(BATCH * SEQ, D_MODEL)
    xl = x_sample.reshape(DEC_BATCH * DEC_SEQ, D_MODEL)
    cond = jnp.concatenate([c_ctx[None, :], c, jnp.zeros((COND_PAD - N_COND, D_MODEL), F32)], axis=0)
    mods = _modulation(cond.T, ada_w, ada_b).reshape(DEPTH, COND_PAD, 6, D_MODEL)
    rope_tables = _rope_tables()
    col_tables = _na_col_tables(na_rpb)
    outs = [[] for _ in range(8)]
    for l in range(DEPTH):
        mod = mods[l]
        lam_init = 0.8 - 0.6 * math.exp(-0.3 * l)
        w_in_bf = w_in[l, :, :Z_WIDTH].astype(BF16)
        wg_bf = w_in[l, :, Z_WIDTH:].astype(BF16)
        wb_bf = w_branch[l].astype(BF16)
        wo_bf = w_out[l].astype(BF16)
        w1_bf = mlp_w1[l].astype(BF16)
        w2_bf = mlp_w2[l].astype(BF16)
        lam_p = jnp.stack([diff_lq1[l], diff_lk1[l], diff_lq2[l], diff_lk2[l]])
        theta = jnp.broadcast_to(jnp.stack([ret_theta_fwd[l], ret_theta_bwd[l]], axis=1)[:, :, None],
                                 (RET_HEADS, 2, RET_DH)).reshape(RET_HEADS // 2, 2, 2, RET_DH)
        vecs = jnp.concatenate([lru_conv_w[l], lru_conv_b[l][None],
                                lru_ba_fwd[l][None], lru_bx_fwd[l][None], lru_lam_fwd[l][None],
                                lru_ba_bwd[l][None], lru_bx_bwd[l][None], lru_lam_bwd[l][None]], axis=0)
        w_gate = jnp.stack([_pair_block_diag(w) for w in
                            (lru_wa_fwd[l], lru_wx_fwd[l], lru_wa_bwd[l], lru_wx_bwd[l])])
        bias = _na_bias_blocks(col_tables[l])

        zc = _inproj(xc, mod, norm_mix_pre[l], w_in_bf, _ctx_mod_row)
        ya = _ctx_na(zc)
        yb = _ctx_diff(zc, lam_p, diff_norm[l], lam_init)
        yc, s_rf, s_rb = _retention(zc, theta, ret_norm[l], BATCH, SEQ)
        yd, h_lf, h_lb = _rglru(zc, vecs, w_gate, BATCH, SEQ)
        xc = _merge(xc, (ya, yb, yc, yd), mod, wg_bf, wb_bf, wo_bf, norm_mix_pre[l], norm_mix_post[l], _ctx_mod_row)
        xc = _mlp(xc, mod, norm_ffn_pre[l], norm_ffn_post[l], w1_bf, w2_bf, _ctx_mod_row)
        z4 = zc.reshape(BATCH, SEQ, Z_WIDTH)
        heads = lambda c0, n, d: z4[:, :, c0:c0 + n * d].reshape(BATCH, SEQ, n, d).transpose(0, 2, 1, 3)
        outs[0].append(heads(COL_NA_K, NA_HEADS, NA_DH))
        outs[1].append(heads(COL_NA_V, NA_HEADS, NA_DH))
        outs[2].append(heads(COL_DF_K, 2 * DIFF_HEADS, DIFF_DH).reshape(BATCH, 2, DIFF_HEADS, SEQ, DIFF_DH))
        outs[3].append(heads(COL_DF_V, DIFF_HEADS, 2 * DIFF_DH))
        outs[4].append(s_rf)
        outs[5].append(s_rb)
        outs[6].append(h_lf.reshape(BATCH, LRU_WIDTH))
        outs[7].append(h_lb.reshape(BATCH, LRU_WIDTH))

        zl = _inproj(xl, mod, norm_mix_pre[l], w_in_bf, _lat_mod_row)
        ya = _lat_na(zl, cache_na_k, cache_na_v, bias, l)
        yb = _lat_diff(zl, cache_diff_k, cache_diff_v, lam_p, diff_norm[l], rope_tables, lam_init, l)
        yc = _retention(zl, theta, ret_norm[l], DEC_BATCH, DEC_SEQ,
                        state=(state_ret_fwd[:, l], state_ret_bwd[:, l]))
        yd = _rglru(zl, vecs, w_gate, DEC_BATCH, DEC_SEQ, state=(state_lru_fwd[:, l], state_lru_bwd[:, l]))
        xl = _merge(xl, (ya, yb, yc, yd), mod, wg_bf, wb_bf, wo_bf, norm_mix_pre[l], norm_mix_post[l], _lat_mod_row)
        xl = _mlp(xl, mod, norm_ffn_pre[l], norm_ffn_post[l], w1_bf, w2_bf, _lat_mod_row)

    stacked = [jnp.stack(o, axis=1) for o in outs]
    return (xc.reshape(BATCH, SEQ, D_MODEL), xl.reshape(DEC_BATCH, DEC_SEQ, D_MODEL), *stacked)
```

```python
import functools
import math

import numpy as np
import jax
import jax.numpy as jnp
from jax import lax
from jax.experimental import pallas as pl
from jax.experimental.pallas import tpu as pltpu

F32 = jnp.float32
BF16 = jnp.bfloat16

D_MODEL = 1024
BATCH = 32
SEQ = 256
DEPTH = 4
DEC_BATCH = 4
DEC_SEQ = 2048
PAST_LEN = 512
GRID_W = 64
NA_HEADS = 8
NA_DH = 64
NA_ROWS = 8
NA_COLS = 16
DIFF_HEADS = 4
DIFF_DH = 64
RET_HEADS = 8
RET_DH = 64
LRU_WIDTH = 512
LRU_BLOCKS = 8
LRU_CONV = 4
LRU_C = 8.0
N_BRANCH = 4
BRANCH_W = 512
D_FF = 4 * D_MODEL
ROPE_BASE = 10000.0
EPS = 1e-6
NEG_INF = -1e30

COL_NA_Q, COL_NA_K, COL_NA_V = 0, 512, 1024
COL_DF_Q, COL_DF_K, COL_DF_V = 1536, 2048, 2560
COL_RT_Q, COL_RT_K, COL_RT_V, COL_RT_G = 3072, 3584, 4096, 4608
COL_LR_X, COL_LR_G = 5120, 5632
COL_GATE = 6144
Z_WIDTH = COL_GATE

N_COND = 1 + DEC_BATCH
COND_PAD = 8
LANES = 128
SUBLANES = 8
VMEM_LIMIT = 56 * 1024 * 1024


def _params(*sem):
    return pltpu.CompilerParams(dimension_semantics=sem, vmem_limit_bytes=VMEM_LIMIT)


def _dot(a, b):
    return jnp.dot(a.astype(BF16), b.astype(BF16), preferred_element_type=F32)


def _dot_nt(a, b):
    return lax.dot_general(a.astype(BF16), b.astype(BF16), (((1,), (1,)), ((), ())),
                           preferred_element_type=F32)


def _dot_tn(a, b):
    return lax.dot_general(a.astype(BF16), b.astype(BF16), (((0,), (0,)), ((), ())),
                           preferred_element_type=F32)


def _rms(x, g):
    return x * lax.rsqrt(jnp.mean(x * x, axis=-1, keepdims=True) + EPS) * g


def _sigmoid(x):
    return 1.0 / (1.0 + jnp.exp(-x))


LOG2E = math.log2(math.e)


def _exp2_weights(parts):
    m = functools.reduce(jnp.maximum, [jnp.max(s, axis=-1, keepdims=True) for s in parts])
    return [jnp.exp2(s - m).astype(BF16) for s in parts]


def _with_ones(v):
    return jnp.concatenate([v.astype(BF16), jnp.ones(v.shape, BF16)], axis=1)


def _normalised(r):
    w = r.shape[1] // 2
    return r[:, :w] / r[:, w:]


def _mod_kernel(ct_ref, w_ref, b_ref, o_ref):
    ct = ct_ref[...]
    s = ct * _sigmoid(ct)
    w = w_ref[0]
    rows = [jnp.sum(s[:, r:r + 1] * w, axis=0, keepdims=True) + b_ref[0] for r in range(N_COND)]
    rows.append(jnp.zeros((COND_PAD - N_COND, w.shape[1]), F32))
    o_ref[0] = jnp.concatenate(rows, axis=0)


def _modulation(cond_t, ada_w, ada_b):
    tn = 768
    n = 6 * D_MODEL
    return pl.pallas_call(
        _mod_kernel,
        grid=(DEPTH, n // tn),
        in_specs=[pl.BlockSpec((D_MODEL, COND_PAD), lambda l, j: (0, 0)),
                  pl.BlockSpec((1, D_MODEL, tn), lambda l, j: (l, 0, j)),
                  pl.BlockSpec((1, 1, tn), lambda l, j: (l, 0, j))],
        out_specs=pl.BlockSpec((1, COND_PAD, tn), lambda l, j: (l, 0, j)),
        out_shape=jax.ShapeDtypeStruct((DEPTH, COND_PAD, n), F32),
        compiler_params=_params("parallel", "parallel"),
        name="modulation",
    )(cond_t, ada_w, ada_b.reshape(DEPTH, 1, n))


def _inproj_kernel(x_ref, m_ref, g_ref, w_ref, z_ref, h_scr):
    @pl.when(pl.program_id(1) == 0)
    def _():
        h = _rms(x_ref[...], g_ref[...]) * (1.0 + m_ref[0, 1:2, :]) + m_ref[0, 0:1, :]
        h_scr[...] = h.astype(BF16)

    z_ref[...] = jnp.dot(h_scr[...], w_ref[...], preferred_element_type=F32)


def _inproj(x, mod, gain, w_bf, mod_row):
    n_tok = x.shape[0]
    tm, tn = 2048, 1024
    return pl.pallas_call(
        _inproj_kernel,
        grid=(n_tok // tm, Z_WIDTH // tn),
        in_specs=[pl.BlockSpec((tm, D_MODEL), lambda i, j: (i, 0)),
                  pl.BlockSpec((1, 6, D_MODEL), lambda i, j: (mod_row(i, tm), 0, 0)),
                  pl.BlockSpec((1, D_MODEL), lambda i, j: (0, 0)),
                  pl.BlockSpec((D_MODEL, tn), lambda i, j: (0, j))],
        out_specs=pl.BlockSpec((tm, tn), lambda i, j: (i, j)),
        out_shape=jax.ShapeDtypeStruct((n_tok, Z_WIDTH), F32),
        scratch_shapes=[pltpu.VMEM((tm, D_MODEL), BF16)],
        compiler_params=_params("parallel", "arbitrary"),
        name="inproj",
    )(x, mod, gain.reshape(1, D_MODEL), w_bf)


def _ctx_na_kernel(q_ref, k_ref, v_ref, o_ref):
    qs = NA_DH ** -0.5 * LOG2E
    first = lax.broadcasted_iota(jnp.int32, (SEQ, LANES), 1) < NA_DH
    for p in range(NA_HEADS * NA_DH // LANES):
        sl = slice(p * LANES, (p + 1) * LANES)
        q = q_ref[:, sl] * qs
        k = k_ref[:, sl].astype(BF16)
        vx = _with_ones(v_ref[:, sl])
        att = []
        for keep in (first, ~first):
            (e,) = _exp2_weights([_dot_nt(jnp.where(keep, q, 0.0), k)])
            att.append(_normalised(jnp.dot(e, vx, preferred_element_type=F32)))
        o_ref[:, sl] = jnp.where(first, att[0], att[1])


def _ctx_na(z):
    blk = lambda j: pl.BlockSpec((SEQ, 512), lambda b: (b, j))
    return pl.pallas_call(
        _ctx_na_kernel,
        grid=(BATCH,),
        in_specs=[blk(COL_NA_Q // 512), blk(COL_NA_K // 512), blk(COL_NA_V // 512)],
        out_specs=pl.BlockSpec((SEQ, 512), lambda b: (b, 0)),
        out_shape=jax.ShapeDtypeStruct((BATCH * SEQ, BRANCH_W), F32),
        compiler_params=_params("parallel"),
        name="ctx_na",
    )(z, z, z)


def _diff_lambda(lam_ref, lam_init):
    a = jnp.sum(lam_ref[0:1, :] * lam_ref[1:2, :], axis=-1, keepdims=True)
    b = jnp.sum(lam_ref[2:3, :] * lam_ref[3:4, :], axis=-1, keepdims=True)
    return jnp.exp(a) - jnp.exp(b) + lam_init


def _ctx_diff_kernel(lam_ref, q_ref, k_ref, v_ref, dn_ref, o_ref, *, lam_init):
    qs = DIFF_DH ** -0.5 * LOG2E
    lam = _diff_lambda(lam_ref, lam_init)
    w = 2 * DIFF_DH
    first = lax.broadcasted_iota(jnp.int32, (SEQ, LANES), 1) < DIFF_DH
    for h in range(DIFF_HEADS):
        vx = _with_ones(v_ref[:, h * w:(h + 1) * w])
        keep = first if h % 2 == 0 else ~first
        att = []
        for c in range(2):
            c0 = (c * DIFF_HEADS + h - h % 2) * DIFF_DH
            sl = slice(c0, c0 + LANES)
            (e,) = _exp2_weights([_dot_nt(jnp.where(keep, q_ref[:, sl] * qs, 0.0), k_ref[:, sl])])
            att.append(_normalised(jnp.dot(e, vx, preferred_element_type=F32)))
        o = att[0] - lam * att[1]
        o_ref[:, h * w:(h + 1) * w] = _rms(o, dn_ref[...]) * (1.0 - lam_init)


def _ctx_diff(z, lam_p, diff_norm, lam_init):
    blk = lambda j: pl.BlockSpec((SEQ, 512), lambda b: (b, j))
    return pl.pallas_call(
        functools.partial(_ctx_diff_kernel, lam_init=lam_init),
        grid=(BATCH,),
        in_specs=[pl.BlockSpec((4, DIFF_DH), lambda b: (0, 0)),
                  blk(COL_DF_Q // 512), blk(COL_DF_K // 512), blk(COL_DF_V // 512),
                  pl.BlockSpec((1, 2 * DIFF_DH), lambda b: (0, 0))],
        out_specs=pl.BlockSpec((SEQ, 512), lambda b: (b, 0)),
        out_shape=jax.ShapeDtypeStruct((BATCH * SEQ, BRANCH_W), F32),
        compiler_params=_params("parallel"),
        name="ctx_diff",
    )(lam_p, z, z, z, diff_norm.reshape(1, 2 * DIFF_DH))


NA_GRID_ROWS = DEC_SEQ // GRID_W
NA_KEY_ROWS = min(NA_ROWS, NA_GRID_ROWS)


NA_DR = 2 * NA_ROWS - 1
NA_DC = 2 * NA_COLS - 1


def _na_col_kernel(rpb_ref, idx_ref, o_ref):
    idx = jnp.broadcast_to(idx_ref[...], o_ref.shape)
    acc = jnp.full(o_ref.shape, NEG_INF, F32)
    for d in range(NA_DC):
        acc = jnp.where(idx == d, jnp.broadcast_to(rpb_ref[:, d:d + 1], o_ref.shape), acc)
    o_ref[...] = acc * LOG2E


def _na_col_tables(na_rpb):
    qc = np.arange(GRID_W)[:, None]
    kc = np.arange(GRID_W)[None, :]
    cstart = np.clip(qc - NA_COLS // 2, 0, GRID_W - NA_COLS)
    ok = (kc >= cstart) & (kc < cstart + NA_COLS)
    idx = np.where(ok, kc - qc + NA_COLS - 1, -1).astype(np.int32).reshape(1, GRID_W * GRID_W)
    rows = NA_HEADS * NA_DR
    rpb = jnp.pad(na_rpb.reshape(DEPTH * rows, NA_DC), ((0, 0), (0, 32 - NA_DC)))
    t = pl.pallas_call(
        _na_col_kernel,
        grid=(DEPTH,),
        in_specs=[pl.BlockSpec((rows, 32), lambda l: (l, 0)),
                  pl.BlockSpec((1, GRID_W * GRID_W), lambda l: (0, 0))],
        out_specs=pl.BlockSpec((rows, GRID_W * GRID_W), lambda l: (l, 0)),
        out_shape=jax.ShapeDtypeStruct((DEPTH * rows, GRID_W * GRID_W), F32),
        compiler_params=_params("parallel"),
        name="na_col_tables",
    )(rpb, jnp.asarray(idx))
    return t.reshape(DEPTH, NA_HEADS, NA_DR, GRID_W, GRID_W)


NA_QROWS = 4
NA_UROWS = 12
NA_QBLK = NA_QROWS * GRID_W
NA_UKEYS = NA_UROWS * GRID_W
NA_NBLK = NA_GRID_ROWS // NA_QROWS


def _na_union_start(r0):
    return int(np.clip(r0 - NA_KEY_ROWS // 2, 0, NA_GRID_ROWS - NA_UROWS))


def _na_pair_tables(t):
    tp = jnp.pad(t, ((0, 0), (0, 0), (1, 1), (0, 0), (0, 0)), constant_values=NEG_INF)
    return jnp.concatenate([tp[:, :, :-1], tp[:, :, 1:]], axis=-1)


def _na_fill_bias(pair_ref, bias_scr):
    first = lax.broadcasted_iota(jnp.int32, (GRID_W, LANES), 1) < GRID_W
    neg = jnp.full((GRID_W, LANES), NEG_INF, F32)
    for variant, r0 in enumerate((0, NA_QROWS, NA_GRID_ROWS - NA_QROWS)):
        us = _na_union_start(r0)
        for j in range(NA_QROWS):
            r = r0 + j
            rs = int(np.clip(r - NA_KEY_ROWS // 2, 0, NA_GRID_ROWS - NA_KEY_ROWS))
            for m in range(NA_UROWS // 2):
                ok = [rs <= us + i < rs + NA_KEY_ROWS for i in (2 * m, 2 * m + 1)]
                p = us + 2 * m - r + NA_ROWS
                for hh in range(2):
                    if ok[0] and ok[1]:
                        tile = pair_ref[0, hh, p]
                    elif ok[0]:
                        tile = jnp.where(first, pair_ref[0, hh, p], NEG_INF)
                    elif ok[1]:
                        tile = jnp.where(first, NEG_INF, pair_ref[0, hh, p])
                    else:
                        tile = neg
                    bias_scr[variant, hh, j * GRID_W:(j + 1) * GRID_W, m * LANES:(m + 1) * LANES] = tile


def _lat_na_kernel(q_ref, k_ref, v_ref, kc_ref, vc_ref, pair_ref, o_ref, k_scr, vx_scr, kc_scr, vcx_scr, bias_ref):
    qs = NA_DH ** -0.5 * LOG2E
    _na_fill_bias(pair_ref, bias_ref)
    k_scr[...] = k_ref[...].astype(BF16)
    vx_scr[...] = _with_ones(v_ref[...])
    kc_scr[...] = jnp.concatenate([kc_ref[0, 0, 0], kc_ref[0, 0, 1]], axis=1).astype(BF16)
    vcx_scr[...] = _with_ones(jnp.concatenate([vc_ref[0, 0, 0], vc_ref[0, 0, 1]], axis=1))
    first = lax.broadcasted_iota(jnp.int32, (NA_QBLK, LANES), 1) < NA_DH

    for blk in range(NA_NBLK):
        us = _na_union_start(blk * NA_QROWS)
        variant = 0 if blk == 0 else (2 if blk == NA_NBLK - 1 else 1)
        rows = slice(blk * NA_QBLK, (blk + 1) * NA_QBLK)
        keys = slice(us * GRID_W, us * GRID_W + NA_UKEYS)
        q = q_ref[rows, :] * qs
        att = []
        for hh, keep in enumerate((first, ~first)):
            qm = jnp.where(keep, q, 0.0).astype(BF16)
            s_loc = _dot_nt(qm, k_scr[keys, :]) + bias_ref[variant, hh]
            s_ctx = _dot_nt(qm, kc_scr[...])
            e_loc, e_ctx = _exp2_weights([s_loc, s_ctx])
            att.append(_normalised(jnp.dot(e_loc, vx_scr[keys, :], preferred_element_type=F32)
                                   + jnp.dot(e_ctx, vcx_scr[...], preferred_element_type=F32)))
        o_ref[rows, :] = jnp.where(first, att[0], att[1])


def _lat_na(z, cache_k, cache_v, pair_tables, l):
    blk = lambda c0: pl.BlockSpec((DEC_SEQ, LANES), lambda p, b: (b, c0 // LANES + p))
    cblk = pl.BlockSpec((1, 1, 2, PAST_LEN, NA_DH), lambda p, b: (b, l, p, 0, 0))
    return pl.pallas_call(
        _lat_na_kernel,
        grid=(NA_HEADS // 2, DEC_BATCH),
        in_specs=[blk(COL_NA_Q), blk(COL_NA_K), blk(COL_NA_V), cblk, cblk,
                  pl.BlockSpec((1, 2, NA_DR + 1, GRID_W, LANES), lambda p, b: (l, p, 0, 0, 0))],
        out_specs=pl.BlockSpec((DEC_SEQ, LANES), lambda p, b: (b, p)),
        out_shape=jax.ShapeDtypeStruct((DEC_BATCH * DEC_SEQ, BRANCH_W), F32),
        scratch_shapes=[pltpu.VMEM((DEC_SEQ, LANES), BF16), pltpu.VMEM((DEC_SEQ, 2 * LANES), BF16),
                        pltpu.VMEM((PAST_LEN, LANES), BF16), pltpu.VMEM((PAST_LEN, 2 * LANES), BF16),
                        pltpu.VMEM((3, 2, NA_QBLK, NA_UKEYS), F32)],
        compiler_params=_params("parallel", "parallel"),
        name="lat_na",
    )(z, z, z, cache_k, cache_v, pair_tables)


def _rope_tables():
    t = jnp.arange(DEC_SEQ)
    quarter = DIFF_DH // 4
    inv = ROPE_BASE ** (-jnp.arange(quarter, dtype=F32) / quarter)
    ang_r = (t // GRID_W).astype(F32)[:, None] * inv
    ang_c = (t % GRID_W).astype(F32)[:, None] * inv
    lane = np.arange(LANES) % DIFF_DH
    idx = lane % quarter
    use_r = jnp.asarray(lane < DIFF_DH // 2)[None, :]
    first = jnp.asarray((lane % (2 * quarter)) < quarter)[None, :]
    ang = jnp.where(use_r, ang_r[:, idx], ang_c[:, idx])
    cos = jnp.cos(ang)
    sin = jnp.sin(ang)
    return cos, jnp.where(first, -sin, 0.0), jnp.where(first, 0.0, sin)


def _rope(x, cos, sin_a, sin_b):
    quarter = DIFF_DH // 4
    up = pltpu.roll(x, LANES - quarter, axis=1)
    down = pltpu.roll(x, quarter, axis=1)
    return x * cos + up * sin_a + down * sin_b


def _lat_diff_kernel(lam_ref, q_ref, k_ref, v_ref, kc_ref, vc_ref, cos_ref, sa_ref, sb_ref, dn_ref, o_ref,
                     k_scr, vx_scr, *, lam_init, tq):
    qs = DIFF_DH ** -0.5 * LOG2E
    n_slab = 512 // LANES
    w = 2 * DIFF_DH
    lat = slice(0, DEC_SEQ)
    ctx = slice(DEC_SEQ, DEC_SEQ + PAST_LEN)

    @pl.when(pl.program_id(1) == 0)
    def _():
        for j in range(n_slab):
            k_scr[j, lat, :] = _rope(k_ref[:, j * LANES:(j + 1) * LANES],
                                     cos_ref[...], sa_ref[...], sb_ref[...]).astype(BF16)
        for c in range(2):
            for hp in range(DIFF_HEADS // 2):
                k_scr[c * (DIFF_HEADS // 2) + hp, ctx, :] = jnp.concatenate(
                    [kc_ref[0, 0, c, 2 * hp], kc_ref[0, 0, c, 2 * hp + 1]], axis=1).astype(BF16)
        for h in range(DIFF_HEADS):
            vx_scr[h, lat, :] = _with_ones(v_ref[:, h * w:(h + 1) * w])
            vx_scr[h, ctx, :] = _with_ones(vc_ref[0, 0, h])

    r0 = pl.multiple_of(pl.program_id(1) * tq, tq)
    cos, sa, sb = cos_ref[pl.ds(r0, tq), :], sa_ref[pl.ds(r0, tq), :], sb_ref[pl.ds(r0, tq), :]
    qrot = [_rope(q_ref[:, j * LANES:(j + 1) * LANES], cos, sa, sb) * qs for j in range(n_slab)]
    lam = _diff_lambda(lam_ref, lam_init)
    first = lax.broadcasted_iota(jnp.int32, (tq, LANES), 1) < DIFF_DH
    for h in range(DIFF_HEADS):
        keep = first if h % 2 == 0 else ~first
        att = []
        for c in range(2):
            j = c * (DIFF_HEADS // 2) + h // 2
            (e,) = _exp2_weights([_dot_nt(jnp.where(keep, qrot[j], 0.0), k_scr[j])])
            att.append(_normalised(jnp.dot(e, vx_scr[h], preferred_element_type=F32)))
        o = att[0] - lam * att[1]
        o_ref[:, h * w:(h + 1) * w] = _rms(o, dn_ref[...]) * (1.0 - lam_init)


def _lat_diff(z, cache_k, cache_v, lam_p, diff_norm, tables, lam_init, l):
    tq = 256
    nq = DEC_SEQ // tq
    tab = pl.BlockSpec((DEC_SEQ, LANES), lambda b, i: (0, 0))
    return pl.pallas_call(
        functools.partial(_lat_diff_kernel, lam_init=lam_init, tq=tq),
        grid=(DEC_BATCH, nq),
        in_specs=[pl.BlockSpec((4, DIFF_DH), lambda b, i: (0, 0)),
                  pl.BlockSpec((tq, 512), lambda b, i: (b * nq + i, COL_DF_Q // 512)),
                  pl.BlockSpec((DEC_SEQ, 512), lambda b, i: (b, COL_DF_K // 512)),
                  pl.BlockSpec((DEC_SEQ, 512), lambda b, i: (b, COL_DF_V // 512)),
                  pl.BlockSpec((1, 1, 2, DIFF_HEADS, PAST_LEN, DIFF_DH), lambda b, i: (b, l, 0, 0, 0, 0)),
                  pl.BlockSpec((1, 1, DIFF_HEADS, PAST_LEN, 2 * DIFF_DH), lambda b, i: (b, l, 0, 0, 0)),
                  tab, tab, tab,
                  pl.BlockSpec((1, 2 * DIFF_DH), lambda b, i: (0, 0))],
        out_specs=pl.BlockSpec((tq, 512), lambda b, i: (b * nq + i, 0)),
        out_shape=jax.ShapeDtypeStruct((DEC_BATCH * DEC_SEQ, BRANCH_W), F32),
        scratch_shapes=[pltpu.VMEM((DIFF_HEADS, DEC_SEQ + PAST_LEN, LANES), BF16),
                        pltpu.VMEM((DIFF_HEADS, DEC_SEQ + PAST_LEN, 4 * DIFF_DH), BF16)],
        compiler_params=_params("parallel", "arbitrary"),
        name="lat_diff",
    )(lam_p, z, z, z, cache_k, cache_v, *tables, diff_norm.reshape(1, 2 * DIFF_DH))


RET_BLOCK = 256


def _block_diag2(a, b):
    zero = jnp.zeros_like(a)
    return jnp.concatenate([jnp.concatenate([a, zero], axis=1), jnp.concatenate([zero, b], axis=1)], axis=0)


def _ret_kernel(*refs, n_chunks, chained):
    th_ref, q_ref, k_ref, v_ref, g_ref, gn_ref = refs[:6]
    if chained:
        s0f_ref, s0b_ref, y_ref = refs[6:]
    else:
        y_ref, sf_ref, sb_ref = refs[6:]
    C = RET_BLOCK
    dh = RET_DH
    rel = (lax.broadcasted_iota(jnp.int32, (C, C), 0) - lax.broadcasted_iota(jnp.int32, (C, C), 1)).astype(F32)
    pos = lax.broadcasted_iota(jnp.int32, (C, LANES), 0).astype(F32)
    first = lax.broadcasted_iota(jnp.int32, (C, LANES), 1) < dh
    lg = jnp.log1p(-jnp.exp(th_ref[0]))
    lgf, lgb = lg[0:1, :], lg[1:2, :]
    decays = [jnp.exp(jnp.where(rel >= 0, rel * lgf[:, c0:c0 + 1], -rel * lgb[:, c0:c0 + 1])) for c0 in (0, dh)]
    xi_f = jnp.exp((pos + 1.0) * lgf)
    zeta_f = jnp.exp((C - 1.0 - pos) * lgf)
    xi_b = jnp.exp((C - pos) * lgb)
    zeta_b = jnp.exp(pos * lgb)
    gn = gn_ref[...]
    rows = [pl.ds(c * C, C) for c in range(n_chunks)]
    ks = [k_ref[r, :] * (dh ** -0.5) for r in rows]
    vs = [v_ref[r, :] for r in rows]
    uf = [_dot_tn(k * zeta_f, v) for k, v in zip(ks, vs)]
    ub = [_dot_tn(k * zeta_b, v) for k, v in zip(ks, vs)]
    if chained:
        diag = ((lax.broadcasted_iota(jnp.int32, (LANES, LANES), 0) < dh)
                == (lax.broadcasted_iota(jnp.int32, (LANES, LANES), 1) < dh))
        gc_f = jnp.exp(C * lgf)
        gc_b = jnp.exp(C * lgb)
        s_f = [_block_diag2(s0f_ref[0, 0], s0f_ref[0, 1])]
        for c in range(n_chunks - 1):
            s_f.append(gc_f * s_f[c] + jnp.where(diag, uf[c], 0.0))
        s_b = [_block_diag2(s0b_ref[0, 0], s0b_ref[0, 1])]
        for c in range(n_chunks - 1, 0, -1):
            s_b.append(gc_b * s_b[-1] + jnp.where(diag, ub[c], 0.0))
        s_b = s_b[::-1]
    for c in range(n_chunks):
        q = q_ref[rows[c], :]
        kb = ks[c].astype(BF16)
        vb = vs[c].astype(BF16)
        o_heads = [jnp.dot((_dot_nt(jnp.where(keep, q, 0.0), kb) * decay).astype(BF16), vb,
                           preferred_element_type=F32) for keep, decay in zip((first, ~first), decays)]
        o = jnp.where(first, o_heads[0], o_heads[1])
        if chained:
            o = o + _dot(q, s_f[c]) * xi_f + _dot(q, s_b[c]) * xi_b
        else:
            for hh in range(2):
                sf_ref[c, hh] = uf[c][hh * dh:(hh + 1) * dh, hh * dh:(hh + 1) * dh]
                sb_ref[c, hh] = ub[c][hh * dh:(hh + 1) * dh, hh * dh:(hh + 1) * dh]
        sq = o * o
        ms = jnp.where(first, jnp.sum(jnp.where(first, sq, 0.0), axis=-1, keepdims=True),
                       jnp.sum(jnp.where(first, 0.0, sq), axis=-1, keepdims=True)) * (1.0 / dh)
        g = g_ref[rows[c], :]
        y_ref[rows[c], :] = o * lax.rsqrt(ms + EPS) * gn * (g * _sigmoid(g))


def _retention(z, theta, ret_norm, n_batch, t_len, state=None):
    chained = state is not None
    n_chunks = t_len // RET_BLOCK if chained else 4
    rows = n_chunks * RET_BLOCK
    n_steps = n_batch * t_len // rows
    blk = lambda c0: pl.BlockSpec((rows, LANES), lambda b, p: (b, c0 // LANES + p))
    in_specs = [pl.BlockSpec((1, 2, LANES), lambda b, p: (p, 0, 0)),
                blk(COL_RT_Q), blk(COL_RT_K), blk(COL_RT_V), blk(COL_RT_G),
                pl.BlockSpec((1, LANES), lambda b, p: (0, p))]
    args = [theta, z, z, z, z, ret_norm.reshape(1, RET_HEADS * RET_DH)]
    y_spec = pl.BlockSpec((rows, LANES), lambda b, p: (b, p))
    y_shape = jax.ShapeDtypeStruct((n_batch * t_len, BRANCH_W), F32)
    if chained:
        sblk = pl.BlockSpec((1, 2, RET_DH, RET_DH), lambda b, p: (b, p, 0, 0))
        in_specs += [sblk, sblk]
        args += list(state)
        out_specs, out_shape = y_spec, y_shape
    else:
        assert t_len == RET_BLOCK
        sblk = pl.BlockSpec((n_chunks, 2, RET_DH, RET_DH), lambda b, p: (b, p, 0, 0))
        s_shape = jax.ShapeDtypeStruct((n_batch, RET_HEADS, RET_DH, RET_DH), F32)
        out_specs, out_shape = [y_spec, sblk, sblk], [y_shape, s_shape, s_shape]
    return pl.pallas_call(
        functools.partial(_ret_kernel, n_chunks=n_chunks, chained=chained),
        grid=(n_steps, RET_HEADS // 2),
        in_specs=in_specs, out_specs=out_specs, out_shape=out_shape,
        compiler_params=_params("parallel", "parallel"),
        name="retention_lat" if chained else "retention_ctx",
    )(*args)


def _shift_rows(x, k, fill, up):
    t_len = x.shape[0]
    if k % SUBLANES == 0:
        pad = jnp.full((k, x.shape[1]), fill, x.dtype)
        return jnp.concatenate([x[k:], pad], axis=0) if up else jnp.concatenate([pad, x[:t_len - k]], axis=0)
    row = lax.broadcasted_iota(jnp.int32, x.shape, 0)
    if up:
        return jnp.where(row < t_len - k, pltpu.roll(x, t_len - k, axis=0), fill)
    return jnp.where(row >= k, pltpu.roll(x, k, axis=0), fill)


def _linear_scan(a, u, up):
    t_len = a.shape[0]
    k = 1
    while k < t_len:
        u = a * _shift_rows(u, k, 0.0, up) + u
        if 2 * k < t_len:
            a = a * _shift_rows(a, k, 1.0, up)
        k *= 2
    return u


def _lru_kernel(*refs, has_state, write_state):
    x_ref, g_ref, vec_ref, w_ref = refs[:4]
    refs = refs[4:]
    if has_state:
        h0f_ref, h0b_ref = refs[:2]
        refs = refs[2:]
    y_ref = refs[0]
    if write_state:
        hf_ref, hb_ref = refs[1:3]
    x = x_ref[...]
    t_len = x.shape[0]
    row = lax.broadcasted_iota(jnp.int32, x.shape, 0)
    vec = lambda i: vec_ref[i:i + 1, :]
    xd = (vec(0) * _shift_rows(x, 1, 0.0, False) + vec(1) * x + vec(2) * _shift_rows(x, 1, 0.0, True)
          + vec(3) * _shift_rows(x, 2, 0.0, True) + vec(4))

    def gates(wa, wx, ba, bx, lam):
        r = _sigmoid(_dot(xd, wa) + ba)
        i = _sigmoid(_dot(xd, wx) + bx)
        nl = -lam
        softplus = jnp.maximum(nl, 0.0) + jnp.log1p(jnp.exp(-jnp.abs(nl)))
        log_a = -LRU_C * r * softplus
        a = jnp.exp(log_a)
        return a, jnp.sqrt(-jnp.tanh(log_a) * (a * a + 1.0)) * (i * xd)

    a, u = gates(w_ref[0, 0], w_ref[1, 0], vec(5), vec(6), vec(7))
    if has_state:
        u = u + jnp.where(row == 0, a * h0f_ref[0], 0.0)
    h_f = _linear_scan(a, u, False)
    a, u = gates(w_ref[2, 0], w_ref[3, 0], vec(8), vec(9), vec(10))
    if has_state:
        u = u + jnp.where(row == t_len - 1, a * h0b_ref[0], 0.0)
    h_b = _linear_scan(a, u, True)
    g = g_ref[...]
    gelu = 0.5 * g * (1.0 + jnp.tanh(math.sqrt(2.0 / math.pi) * (g + 0.044715 * (g * g * g))))
    y_ref[...] = (h_f + h_b) * gelu
    if write_state:
        hf_ref[0] = h_f[t_len - 1:t_len, :]
        hb_ref[0] = h_b[0:1, :]


def _rglru(z, vecs, w_gate, n_batch, t_len, state=None):
    blk = lambda c0: pl.BlockSpec((t_len, LANES), lambda b, j: (b, c0 // LANES + j))
    hblk = pl.BlockSpec((1, 1, LANES), lambda b, j: (b, 0, j))
    in_specs = [blk(COL_LR_X), blk(COL_LR_G),
                pl.BlockSpec((11, LANES), lambda b, j: (0, j)),
                pl.BlockSpec((4, 1, LANES, LANES), lambda b, j: (0, j, 0, 0))]
    args = [z, z, vecs, w_gate]
    y_spec = pl.BlockSpec((t_len, LANES), lambda b, j: (b, j))
    y_shape = jax.ShapeDtypeStruct((n_batch * t_len, BRANCH_W), F32)
    if state is not None:
        in_specs += [hblk, hblk]
        args += [s.reshape(n_batch, 1, LRU_WIDTH) for s in state]
        out_specs, out_shape = y_spec, y_shape
    else:
        h_shape = jax.ShapeDtypeStruct((n_batch, 1, LRU_WIDTH), F32)
        out_specs, out_shape = [y_spec, hblk, hblk], [y_shape, h_shape, h_shape]
    return pl.pallas_call(
        functools.partial(_lru_kernel, has_state=state is not None, write_state=state is None),
        grid=(n_batch, LRU_WIDTH // LANES),
        in_specs=in_specs, out_specs=out_specs, out_shape=out_shape,
        compiler_params=_params("parallel", "parallel"),
        name="rglru_lat" if state is not None else "rglru_ctx",
    )(*args)


def _pair_block_diag(w):
    bw = LRU_WIDTH // LRU_BLOCKS
    w = w.reshape(LRU_BLOCKS // 2, 2, bw, bw)
    zero = jnp.zeros_like(w[:, 0])
    top = jnp.concatenate([w[:, 0], zero], axis=2)
    bot = jnp.concatenate([zero, w[:, 1]], axis=2)
    return jnp.concatenate([top, bot], axis=1)


def _merge_kernel(x_ref, ya_ref, yb_ref, yc_ref, yd_ref, wg_ref, wb_ref, wo_ref, gpre_ref, gpost_ref, m_ref, o_ref):
    x = x_ref[...]
    h = (_rms(x, gpre_ref[...]) * (1.0 + m_ref[0, 1:2, :]) + m_ref[0, 0:1, :]).astype(BF16)
    acc = None
    for k, y_ref in enumerate((ya_ref, yb_ref, yc_ref, yd_ref)):
        logits = jnp.dot(h, wg_ref[:, k * D_MODEL:(k + 1) * D_MODEL], preferred_element_type=F32)
        term = _sigmoid(logits) * jnp.dot(y_ref[...].astype(BF16), wb_ref[k], preferred_element_type=F32)
        acc = term if acc is None else acc + term
    out = jnp.dot(acc.astype(BF16), wo_ref[...], preferred_element_type=F32)
    o_ref[...] = x + m_ref[0, 2:3, :] * _rms(out, gpost_ref[...])


def _merge(x, ys, mod, wg_bf, wb_bf, wo_bf, gpre, gpost, mod_row):
    n_tok = x.shape[0]
    tm = 256
    yblk = pl.BlockSpec((tm, BRANCH_W), lambda i: (i, 0))
    return pl.pallas_call(
        _merge_kernel,
        grid=(n_tok // tm,),
        in_specs=[pl.BlockSpec((tm, D_MODEL), lambda i: (i, 0)), yblk, yblk, yblk, yblk,
                  pl.BlockSpec((D_MODEL, N_BRANCH * D_MODEL), lambda i: (0, 0)),
                  pl.BlockSpec((N_BRANCH, BRANCH_W, D_MODEL), lambda i: (0, 0, 0)),
                  pl.BlockSpec((D_MODEL, D_MODEL), lambda i: (0, 0)),
                  pl.BlockSpec((1, D_MODEL), lambda i: (0, 0)),
                  pl.BlockSpec((1, D_MODEL), lambda i: (0, 0)),
                  pl.BlockSpec((1, 6, D_MODEL), lambda i: (mod_row(i, tm), 0, 0))],
        out_specs=pl.BlockSpec((tm, D_MODEL), lambda i: (i, 0)),
        out_shape=jax.ShapeDtypeStruct((n_tok, D_MODEL), F32),
        compiler_params=_params("parallel"),
        name="merge",
    )(x, *ys, wg_bf, wb_bf, wo_bf, gpre.reshape(1, D_MODEL), gpost.reshape(1, D_MODEL), mod)


def _mlp_kernel(x_ref, m_ref, gpre_ref, gpost_ref, w1_ref, w2_ref, o_ref):
    x = x_ref[...]
    h = (_rms(x, gpre_ref[...]) * (1.0 + m_ref[0, 4:5, :]) + m_ref[0, 3:4, :]).astype(BF16)
    ff_chunk = D_MODEL
    y = None
    for j in range(D_FF // ff_chunk):
        sl = slice(j * ff_chunk, (j + 1) * ff_chunk)
        a = jnp.maximum(jnp.dot(h, w1_ref[:, sl], preferred_element_type=F32), 0.0)
        part = jnp.dot((a * a).astype(BF16), w2_ref[sl, :], preferred_element_type=F32)
        y = part if y is None else y + part
    o_ref[...] = x + m_ref[0, 5:6, :] * _rms(y, gpost_ref[...])


def _mlp(x, mod, gpre, gpost, w1_bf, w2_bf, mod_row):
    n_tok = x.shape[0]
    tm = 256
    return pl.pallas_call(
        _mlp_kernel,
        grid=(n_tok // tm,),
        in_specs=[pl.BlockSpec((tm, D_MODEL), lambda i: (i, 0)),
                  pl.BlockSpec((1, 6, D_MODEL), lambda i: (mod_row(i, tm), 0, 0)),
                  pl.BlockSpec((1, D_MODEL), lambda i: (0, 0)),
                  pl.BlockSpec((1, D_MODEL), lambda i: (0, 0)),
                  pl.BlockSpec((D_MODEL, D_FF), lambda i: (0, 0)),
                  pl.BlockSpec((D_FF, D_MODEL), lambda i: (0, 0))],
        out_specs=pl.BlockSpec((tm, D_MODEL), lambda i: (i, 0)),
        out_shape=jax.ShapeDtypeStruct((n_tok, D_MODEL), F32),
        compiler_params=_params("parallel"),
        name="mlp",
    )(x, mod, gpre.reshape(1, D_MODEL), gpost.reshape(1, D_MODEL), w1_bf, w2_bf)


def _ctx_mod_row(i, tm):
    return 0


def _lat_mod_row(i, tm):
    return 1 + i // (DEC_SEQ // tm)


def kernel(x_prompt, x_sample, cache_na_k, cache_na_v, cache_diff_k, cache_diff_v, state_ret_fwd, state_ret_bwd, state_lru_fwd, state_lru_bwd, c, c_ctx, ada_w, ada_b, norm_mix_pre, norm_mix_post, norm_ffn_pre, norm_ffn_post, w_in, na_rpb, diff_lq1, diff_lk1, diff_lq2, diff_lk2, diff_norm, ret_theta_fwd, ret_theta_bwd, ret_norm, lru_conv_w, lru_conv_b, lru_wa_fwd, lru_ba_fwd, lru_wx_fwd, lru_bx_fwd, lru_lam_fwd, lru_wa_bwd, lru_ba_bwd, lru_wx_bwd, lru_bx_bwd, lru_lam_bwd, w_branch, w_out, mlp_w1, mlp_w2):
    xc = x_prompt.reshape(BATCH * SEQ, D_MODEL)
    xl = x_sample.reshape(DEC_BATCH * DEC_SEQ, D_MODEL)
    cond = jnp.concatenate([c_ctx[None, :], c, jnp.zeros((COND_PAD - N_COND, D_MODEL), F32)], axis=0)
    mods = _modulation(cond.T, ada_w, ada_b).reshape(DEPTH, COND_PAD, 6, D_MODEL)
    rope_tables = _rope_tables()
    pair_tables = _na_pair_tables(_na_col_tables(na_rpb))
    outs = [[] for _ in range(8)]
    for l in range(DEPTH):
        mod = mods[l]
        lam_init = 0.8 - 0.6 * math.exp(-0.3 * l)
        w_in_bf = w_in[l, :, :Z_WIDTH].astype(BF16)
        wg_bf = w_in[l, :, Z_WIDTH:].astype(BF16)
        wb_bf = w_branch[l].astype(BF16)
        wo_bf = w_out[l].astype(BF16)
        w1_bf = mlp_w1[l].astype(BF16)
        w2_bf = mlp_w2[l].astype(BF16)
        lam_p = jnp.stack([diff_lq1[l], diff_lk1[l], diff_lq2[l], diff_lk2[l]])
        theta = jnp.stack([jnp.repeat(t, RET_DH).reshape(RET_HEADS // 2, LANES)
                           for t in (ret_theta_fwd[l], ret_theta_bwd[l])], axis=1)
        vecs = jnp.concatenate([lru_conv_w[l], lru_conv_b[l][None],
                                lru_ba_fwd[l][None], lru_bx_fwd[l][None], lru_lam_fwd[l][None],
                                lru_ba_bwd[l][None], lru_bx_bwd[l][None], lru_lam_bwd[l][None]], axis=0)
        w_gate = jnp.stack([_pair_block_diag(w) for w in
                            (lru_wa_fwd[l], lru_wx_fwd[l], lru_wa_bwd[l], lru_wx_bwd[l])])

        zc = _inproj(xc, mod, norm_mix_pre[l], w_in_bf, _ctx_mod_row)
        ya = _ctx_na(zc)
        yb = _ctx_diff(zc, lam_p, diff_norm[l], lam_init)
        yc, s_rf, s_rb = _retention(zc, theta, ret_norm[l], BATCH, SEQ)
        yd, h_lf, h_lb = _rglru(zc, vecs, w_gate, BATCH, SEQ)
        xc = _merge(xc, (ya, yb, yc, yd), mod, wg_bf, wb_bf, wo_bf, norm_mix_pre[l], norm_mix_post[l], _ctx_mod_row)
        xc = _mlp(xc, mod, norm_ffn_pre[l], norm_ffn_post[l], w1_bf, w2_bf, _ctx_mod_row)
        z4 = zc.reshape(BATCH, SEQ, Z_WIDTH)
        heads = lambda c0, n, d: z4[:, :, c0:c0 + n * d].reshape(BATCH, SEQ, n, d).transpose(0, 2, 1, 3)
        outs[0].append(heads(COL_NA_K, NA_HEADS, NA_DH))
        outs[1].append(heads(COL_NA_V, NA_HEADS, NA_DH))
        outs[2].append(heads(COL_DF_K, 2 * DIFF_HEADS, DIFF_DH).reshape(BATCH, 2, DIFF_HEADS, SEQ, DIFF_DH))
        outs[3].append(heads(COL_DF_V, DIFF_HEADS, 2 * DIFF_DH))
        outs[4].append(s_rf)
        outs[5].append(s_rb)
        outs[6].append(h_lf.reshape(BATCH, LRU_WIDTH))
        outs[7].append(h_lb.reshape(BATCH, LRU_WIDTH))

        zl = _inproj(xl, mod, norm_mix_pre[l], w_in_bf, _lat_mod_row)
        ya = _lat_na(zl, cache_na_k, cache_na_v, pair_tables, l)
        yb = _lat_diff(zl, cache_diff_k, cache_diff_v, lam_p, diff_norm[l], rope_tables, lam_init, l)
        yc = _retention(zl, theta, ret_norm[l], DEC_BATCH, DEC_SEQ,
                        state=(state_ret_fwd[:, l], state_ret_bwd[:, l]))
        yd = _rglru(zl, vecs, w_gate, DEC_BATCH, DEC_SEQ, state=(state_lru_fwd[:, l], state_lru_bwd[:, l]))
        xl = _merge(xl, (ya, yb, yc, yd), mod, wg_bf, wb_bf, wo_bf, norm_mix_pre[l], norm_mix_post[l], _lat_mod_row)
        xl = _mlp(xl, mod, norm_ffn_pre[l], norm_ffn_post[l], w1_bf, w2_bf, _lat_mod_row)

    stacked = [jnp.stack(o, axis=1) for o in outs]
    return (xc.reshape(BATCH, SEQ, D_MODEL), xl.reshape(DEC_BATCH, DEC_SEQ, D_MODEL), *stacked)
```

```python
import functools
import math

import numpy as np
import jax
import jax.numpy as jnp
from jax import lax
from jax.experimental import pallas as pl
from jax.experimental.pallas import tpu as pltpu

F32 = jnp.float32
BF16 = jnp.bfloat16

D_MODEL = 1024
BATCH = 32
SEQ = 256
DEPTH = 4
DEC_BATCH = 4
DEC_SEQ = 2048
PAST_LEN = 512
GRID_W = 64
NA_HEADS = 8
NA_DH = 64
NA_ROWS = 8
NA_COLS = 16
DIFF_HEADS = 4
DIFF_DH = 64
RET_HEADS = 8
RET_DH = 64
LRU_WIDTH = 512
LRU_BLOCKS = 8
LRU_CONV = 4
LRU_C = 8.0
N_BRANCH = 4
BRANCH_W = 512
D_FF = 4 * D_MODEL
ROPE_BASE = 10000.0
EPS = 1e-6
NEG_INF = -1e30

COL_NA_Q, COL_NA_K, COL_NA_V = 0, 512, 1024
COL_DF_Q, COL_DF_K, COL_DF_V = 1536, 2048, 2560
COL_RT_Q, COL_RT_K, COL_RT_V, COL_RT_G = 3072, 3584, 4096, 4608
COL_LR_X, COL_LR_G = 5120, 5632
COL_GATE = 6144
Z_WIDTH = COL_GATE

N_COND = 1 + DEC_BATCH
COND_PAD = 8
LANES = 128
SUBLANES = 8
VMEM_LIMIT = 56 * 1024 * 1024


def _params(*sem):
    return pltpu.CompilerParams(dimension_semantics=sem, vmem_limit_bytes=VMEM_LIMIT)


def _dot(a, b):
    return jnp.dot(a.astype(BF16), b.astype(BF16), preferred_element_type=F32)


def _dot_nt(a, b):
    return lax.dot_general(a.astype(BF16), b.astype(BF16), (((1,), (1,)), ((), ())),
                           preferred_element_type=F32)


def _dot_tn(a, b):
    return lax.dot_general(a.astype(BF16), b.astype(BF16), (((0,), (0,)), ((), ())),
                           preferred_element_type=F32)


def _rms(x, g):
    return x * lax.rsqrt(jnp.mean(x * x, axis=-1, keepdims=True) + EPS) * g


def _sigmoid(x):
    return 1.0 / (1.0 + jnp.exp(-x))


LOG2E = math.log2(math.e)


def _exp2_weights(parts):
    m = functools.reduce(jnp.maximum, [jnp.max(s, axis=-1, keepdims=True) for s in parts])
    return [jnp.exp2(s - m).astype(BF16) for s in parts]


def _with_ones(v):
    return jnp.concatenate([v.astype(BF16), jnp.ones(v.shape, BF16)], axis=1)


def _normalised(r):
    w = r.shape[1] // 2
    return r[:, :w] / r[:, w:]


def _mod_kernel(ct_ref, w_ref, b_ref, o_ref):
    ct = ct_ref[...]
    s = ct * _sigmoid(ct)
    w = w_ref[0]
    rows = [jnp.sum(s[:, r:r + 1] * w, axis=0, keepdims=True) + b_ref[0] for r in range(N_COND)]
    rows.append(jnp.zeros((COND_PAD - N_COND, w.shape[1]), F32))
    o_ref[0] = jnp.concatenate(rows, axis=0)


def _modulation(cond_t, ada_w, ada_b):
    tn = 768
    n = 6 * D_MODEL
    return pl.pallas_call(
        _mod_kernel,
        grid=(DEPTH, n // tn),
        in_specs=[pl.BlockSpec((D_MODEL, COND_PAD), lambda l, j: (0, 0)),
                  pl.BlockSpec((1, D_MODEL, tn), lambda l, j: (l, 0, j)),
                  pl.BlockSpec((1, 1, tn), lambda l, j: (l, 0, j))],
        out_specs=pl.BlockSpec((1, COND_PAD, tn), lambda l, j: (l, 0, j)),
        out_shape=jax.ShapeDtypeStruct((DEPTH, COND_PAD, n), F32),
        compiler_params=_params("parallel", "parallel"),
        name="modulation",
    )(cond_t, ada_w, ada_b.reshape(DEPTH, 1, n))


def _inproj_kernel(x_ref, m_ref, g_ref, w_ref, z_ref, h_scr):
    @pl.when(pl.program_id(1) == 0)
    def _():
        h = _rms(x_ref[...], g_ref[...]) * (1.0 + m_ref[0, 1:2, :]) + m_ref[0, 0:1, :]
        h_scr[...] = h.astype(BF16)

    z_ref[...] = jnp.dot(h_scr[...], w_ref[...], preferred_element_type=F32)


def _inproj(x, mod, gain, w_bf, mod_row):
    n_tok = x.shape[0]
    tm, tn = 2048, 1024
    return pl.pallas_call(
        _inproj_kernel,
        grid=(n_tok // tm, Z_WIDTH // tn),
        in_specs=[pl.BlockSpec((tm, D_MODEL), lambda i, j: (i, 0)),
                  pl.BlockSpec((1, 6, D_MODEL), lambda i, j: (mod_row(i, tm), 0, 0)),
                  pl.BlockSpec((1, D_MODEL), lambda i, j: (0, 0)),
                  pl.BlockSpec((D_MODEL, tn), lambda i, j: (0, j))],
        out_specs=pl.BlockSpec((tm, tn), lambda i, j: (i, j)),
        out_shape=jax.ShapeDtypeStruct((n_tok, Z_WIDTH), F32),
        scratch_shapes=[pltpu.VMEM((tm, D_MODEL), BF16)],
        compiler_params=_params("parallel", "arbitrary"),
        name="inproj",
    )(x, mod, gain.reshape(1, D_MODEL), w_bf)


def _ctx_na_kernel(q_ref, k_ref, v_ref, o_ref):
    qs = NA_DH ** -0.5 * LOG2E
    first = lax.broadcasted_iota(jnp.int32, (SEQ, LANES), 1) < NA_DH
    for p in range(NA_HEADS * NA_DH // LANES):
        sl = slice(p * LANES, (p + 1) * LANES)
        q = q_ref[:, sl] * qs
        k = k_ref[:, sl].astype(BF16)
        vx = _with_ones(v_ref[:, sl])
        att = []
        for keep in (first, ~first):
            (e,) = _exp2_weights([_dot_nt(jnp.where(keep, q, 0.0), k)])
            att.append(_normalised(jnp.dot(e, vx, preferred_element_type=F32)))
        o_ref[:, sl] = jnp.where(first, att[0], att[1])


def _ctx_na(z):
    blk = lambda j: pl.BlockSpec((SEQ, 512), lambda b: (b, j))
    return pl.pallas_call(
        _ctx_na_kernel,
        grid=(BATCH,),
        in_specs=[blk(COL_NA_Q // 512), blk(COL_NA_K // 512), blk(COL_NA_V // 512)],
        out_specs=pl.BlockSpec((SEQ, 512), lambda b: (b, 0)),
        out_shape=jax.ShapeDtypeStruct((BATCH * SEQ, BRANCH_W), F32),
        compiler_params=_params("parallel"),
        name="ctx_na",
    )(z, z, z)


def _diff_lambda(lam_ref, lam_init):
    a = jnp.sum(lam_ref[0:1, :] * lam_ref[1:2, :], axis=-1, keepdims=True)
    b = jnp.sum(lam_ref[2:3, :] * lam_ref[3:4, :], axis=-1, keepdims=True)
    return jnp.exp(a) - jnp.exp(b) + lam_init


def _ctx_diff_kernel(lam_ref, q_ref, k_ref, v_ref, dn_ref, o_ref, *, lam_init):
    qs = DIFF_DH ** -0.5 * LOG2E
    lam = _diff_lambda(lam_ref, lam_init)
    w = 2 * DIFF_DH
    first = lax.broadcasted_iota(jnp.int32, (SEQ, LANES), 1) < DIFF_DH
    for h in range(DIFF_HEADS):
        vx = _with_ones(v_ref[:, h * w:(h + 1) * w])
        keep = first if h % 2 == 0 else ~first
        att = []
        for c in range(2):
            c0 = (c * DIFF_HEADS + h - h % 2) * DIFF_DH
            sl = slice(c0, c0 + LANES)
            (e,) = _exp2_weights([_dot_nt(jnp.where(keep, q_ref[:, sl] * qs, 0.0), k_ref[:, sl])])
            att.append(_normalised(jnp.dot(e, vx, preferred_element_type=F32)))
        o = att[0] - lam * att[1]
        o_ref[:, h * w:(h + 1) * w] = _rms(o, dn_ref[...]) * (1.0 - lam_init)


def _ctx_diff(z, lam_p, diff_norm, lam_init):
    blk = lambda j: pl.BlockSpec((SEQ, 512), lambda b: (b, j))
    return pl.pallas_call(
        functools.partial(_ctx_diff_kernel, lam_init=lam_init),
        grid=(BATCH,),
        in_specs=[pl.BlockSpec((4, DIFF_DH), lambda b: (0, 0)),
                  blk(COL_DF_Q // 512), blk(COL_DF_K // 512), blk(COL_DF_V // 512),
                  pl.BlockSpec((1, 2 * DIFF_DH), lambda b: (0, 0))],
        out_specs=pl.BlockSpec((SEQ, 512), lambda b: (b, 0)),
        out_shape=jax.ShapeDtypeStruct((BATCH * SEQ, BRANCH_W), F32),
        compiler_params=_params("parallel"),
        name="ctx_diff",
    )(lam_p, z, z, z, diff_norm.reshape(1, 2 * DIFF_DH))


NA_GRID_ROWS = DEC_SEQ // GRID_W
NA_KEY_ROWS = min(NA_ROWS, NA_GRID_ROWS)


NA_DR = 2 * NA_ROWS - 1
NA_DC = 2 * NA_COLS - 1


def _na_col_kernel(rpb_ref, idx_ref, o_ref):
    idx = jnp.broadcast_to(idx_ref[...], o_ref.shape)
    acc = jnp.full(o_ref.shape, NEG_INF, F32)
    for d in range(NA_DC):
        acc = jnp.where(idx == d, jnp.broadcast_to(rpb_ref[:, d:d + 1], o_ref.shape), acc)
    o_ref[...] = acc * LOG2E


def _na_col_tables(na_rpb):
    qc = np.arange(GRID_W)[:, None]
    kc = np.arange(GRID_W)[None, :]
    cstart = np.clip(qc - NA_COLS // 2, 0, GRID_W - NA_COLS)
    ok = (kc >= cstart) & (kc < cstart + NA_COLS)
    idx = np.where(ok, kc - qc + NA_COLS - 1, -1).astype(np.int32).reshape(1, GRID_W * GRID_W)
    rows = NA_HEADS * NA_DR
    rpb = jnp.pad(na_rpb.reshape(DEPTH * rows, NA_DC), ((0, 0), (0, 32 - NA_DC)))
    t = pl.pallas_call(
        _na_col_kernel,
        grid=(DEPTH,),
        in_specs=[pl.BlockSpec((rows, 32), lambda l: (l, 0)),
                  pl.BlockSpec((1, GRID_W * GRID_W), lambda l: (0, 0))],
        out_specs=pl.BlockSpec((rows, GRID_W * GRID_W), lambda l: (l, 0)),
        out_shape=jax.ShapeDtypeStruct((DEPTH * rows, GRID_W * GRID_W), F32),
        compiler_params=_params("parallel"),
        name="na_col_tables",
    )(rpb, jnp.asarray(idx))
    return t.reshape(DEPTH, NA_HEADS, NA_DR, GRID_W, GRID_W)


NA_QROWS = 4
NA_UROWS = 12
NA_QBLK = NA_QROWS * GRID_W
NA_UKEYS = NA_UROWS * GRID_W
NA_NBLK = NA_GRID_ROWS // NA_QROWS


def _na_union_start(r0):
    return int(np.clip(r0 - NA_KEY_ROWS // 2, 0, NA_GRID_ROWS - NA_UROWS))


def _na_pair_tables(t):
    tp = jnp.pad(t, ((0, 0), (0, 0), (1, 1), (0, 0), (0, 0)), constant_values=NEG_INF)
    return jnp.concatenate([tp[:, :, :-1], tp[:, :, 1:]], axis=-1)


def _na_fill_bias(pair_ref, bias_scr):
    first = lax.broadcasted_iota(jnp.int32, (GRID_W, LANES), 1) < GRID_W
    neg = jnp.full((GRID_W, LANES), NEG_INF, F32)
    for variant, r0 in enumerate((0, NA_QROWS, NA_GRID_ROWS - NA_QROWS)):
        us = _na_union_start(r0)
        for j in range(NA_QROWS):
            r = r0 + j
            rs = int(np.clip(r - NA_KEY_ROWS // 2, 0, NA_GRID_ROWS - NA_KEY_ROWS))
            for m in range(NA_UROWS // 2):
                ok = [rs <= us + i < rs + NA_KEY_ROWS for i in (2 * m, 2 * m + 1)]
                p = us + 2 * m - r + NA_ROWS
                for hh in range(2):
                    if ok[0] and ok[1]:
                        tile = pair_ref[0, hh, p]
                    elif ok[0]:
                        tile = jnp.where(first, pair_ref[0, hh, p], NEG_INF)
                    elif ok[1]:
                        tile = jnp.where(first, NEG_INF, pair_ref[0, hh, p])
                    else:
                        tile = neg
                    bias_scr[variant, hh, j * GRID_W:(j + 1) * GRID_W, m * LANES:(m + 1) * LANES] = tile


def _lat_na_kernel(q_ref, k_ref, v_ref, kc_ref, vc_ref, pair_ref, o_ref, k_scr, vx_scr, kc_scr, vcx_scr, bias_ref):
    qs = NA_DH ** -0.5 * LOG2E
    _na_fill_bias(pair_ref, bias_ref)
    k_scr[...] = k_ref[...].astype(BF16)
    vx_scr[...] = _with_ones(v_ref[...])
    kc_scr[...] = jnp.concatenate([kc_ref[0, 0, 0], kc_ref[0, 0, 1]], axis=1).astype(BF16)
    vcx_scr[...] = _with_ones(jnp.concatenate([vc_ref[0, 0, 0], vc_ref[0, 0, 1]], axis=1))
    first = lax.broadcasted_iota(jnp.int32, (NA_QBLK, LANES), 1) < NA_DH

    for blk in range(NA_NBLK):
        us = _na_union_start(blk * NA_QROWS)
        variant = 0 if blk == 0 else (2 if blk == NA_NBLK - 1 else 1)
        rows = slice(blk * NA_QBLK, (blk + 1) * NA_QBLK)
        keys = slice(us * GRID_W, us * GRID_W + NA_UKEYS)
        q = q_ref[rows, :] * qs
        att = []
        for hh, keep in enumerate((first, ~first)):
            qm = jnp.where(keep, q, 0.0).astype(BF16)
            s_loc = _dot_nt(qm, k_scr[keys, :]) + bias_ref[variant, hh]
            s_ctx = _dot_nt(qm, kc_scr[...])
            e_loc, e_ctx = _exp2_weights([s_loc, s_ctx])
            att.append(_normalised(jnp.dot(e_loc, vx_scr[keys, :], preferred_element_type=F32)
                                   + jnp.dot(e_ctx, vcx_scr[...], preferred_element_type=F32)))
        o_ref[rows, :] = jnp.where(first, att[0], att[1])


def _lat_na(z, cache_k, cache_v, pair_tables, l):
    blk = lambda c0: pl.BlockSpec((DEC_SEQ, LANES), lambda p, b: (b, c0 // LANES + p))
    cblk = pl.BlockSpec((1, 1, 2, PAST_LEN, NA_DH), lambda p, b: (b, l, p, 0, 0))
    return pl.pallas_call(
        _lat_na_kernel,
        grid=(NA_HEADS // 2, DEC_BATCH),
        in_specs=[blk(COL_NA_Q), blk(COL_NA_K), blk(COL_NA_V), cblk, cblk,
                  pl.BlockSpec((1, 2, NA_DR + 1, GRID_W, LANES), lambda p, b: (l, p, 0, 0, 0))],
        out_specs=pl.BlockSpec((DEC_SEQ, LANES), lambda p, b: (b, p)),
        out_shape=jax.ShapeDtypeStruct((DEC_BATCH * DEC_SEQ, BRANCH_W), F32),
        scratch_shapes=[pltpu.VMEM((DEC_SEQ, LANES), BF16), pltpu.VMEM((DEC_SEQ, 2 * LANES), BF16),
                        pltpu.VMEM((PAST_LEN, LANES), BF16), pltpu.VMEM((PAST_LEN, 2 * LANES), BF16),
                        pltpu.VMEM((3, 2, NA_QBLK, NA_UKEYS), F32)],
        compiler_params=_params("parallel", "parallel"),
        name="lat_na",
    )(z, z, z, cache_k, cache_v, pair_tables)


def _rope_tables():
    t = jnp.arange(DEC_SEQ)
    quarter = DIFF_DH // 4
    inv = ROPE_BASE ** (-jnp.arange(quarter, dtype=F32) / quarter)
    ang_r = (t // GRID_W).astype(F32)[:, None] * inv
    ang_c = (t % GRID_W).astype(F32)[:, None] * inv
    lane = np.arange(LANES) % DIFF_DH
    idx = lane % quarter
    use_r = jnp.asarray(lane < DIFF_DH // 2)[None, :]
    first = jnp.asarray((lane % (2 * quarter)) < quarter)[None, :]
    ang = jnp.where(use_r, ang_r[:, idx], ang_c[:, idx])
    cos = jnp.cos(ang)
    sin = jnp.sin(ang)
    return cos, jnp.where(first, -sin, 0.0), jnp.where(first, 0.0, sin)


def _rope(x, cos, sin_a, sin_b):
    quarter = DIFF_DH // 4
    up = pltpu.roll(x, LANES - quarter, axis=1)
    down = pltpu.roll(x, quarter, axis=1)
    return x * cos + up * sin_a + down * sin_b


def _lat_diff_kernel(lam_ref, q_ref, k_ref, v_ref, kc_ref, vc_ref, cos_ref, sa_ref, sb_ref, dn_ref, o_ref,
                     k_scr, vx_scr, *, lam_init, tq):
    qs = DIFF_DH ** -0.5 * LOG2E
    n_slab = 512 // LANES
    w = 2 * DIFF_DH
    lat = slice(0, DEC_SEQ)
    ctx = slice(DEC_SEQ, DEC_SEQ + PAST_LEN)

    @pl.when(pl.program_id(1) == 0)
    def _():
        for j in range(n_slab):
            k_scr[j, lat, :] = _rope(k_ref[:, j * LANES:(j + 1) * LANES],
                                     cos_ref[...], sa_ref[...], sb_ref[...]).astype(BF16)
        for c in range(2):
            for hp in range(DIFF_HEADS // 2):
                k_scr[c * (DIFF_HEADS // 2) + hp, ctx, :] = jnp.concatenate(
                    [kc_ref[0, 0, c, 2 * hp], kc_ref[0, 0, c, 2 * hp + 1]], axis=1).astype(BF16)
        for h in range(DIFF_HEADS):
            vx_scr[h, lat, :] = _with_ones(v_ref[:, h * w:(h + 1) * w])
            vx_scr[h, ctx, :] = _with_ones(vc_ref[0, 0, h])

    r0 = pl.multiple_of(pl.program_id(1) * tq, tq)
    cos, sa, sb = cos_ref[pl.ds(r0, tq), :], sa_ref[pl.ds(r0, tq), :], sb_ref[pl.ds(r0, tq), :]
    qrot = [_rope(q_ref[:, j * LANES:(j + 1) * LANES], cos, sa, sb) * qs for j in range(n_slab)]
    lam = _diff_lambda(lam_ref, lam_init)
    first = lax.broadcasted_iota(jnp.int32, (tq, LANES), 1) < DIFF_DH
    for h in range(DIFF_HEADS):
        keep = first if h % 2 == 0 else ~first
        att = []
        for c in range(2):
            j = c * (DIFF_HEADS // 2) + h // 2
            (e,) = _exp2_weights([_dot_nt(jnp.where(keep, qrot[j], 0.0), k_scr[j])])
            att.append(_normalised(jnp.dot(e, vx_scr[h], preferred_element_type=F32)))
        o = att[0] - lam * att[1]
        o_ref[:, h * w:(h + 1) * w] = _rms(o, dn_ref[...]) * (1.0 - lam_init)


def _lat_diff(z, cache_k, cache_v, lam_p, diff_norm, tables, lam_init, l):
    tq = 256
    nq = DEC_SEQ // tq
    tab = pl.BlockSpec((DEC_SEQ, LANES), lambda b, i: (0, 0))
    return pl.pallas_call(
        functools.partial(_lat_diff_kernel, lam_init=lam_init, tq=tq),
        grid=(DEC_BATCH, nq),
        in_specs=[pl.BlockSpec((4, DIFF_DH), lambda b, i: (0, 0)),
                  pl.BlockSpec((tq, 512), lambda b, i: (b * nq + i, COL_DF_Q // 512)),
                  pl.BlockSpec((DEC_SEQ, 512), lambda b, i: (b, COL_DF_K // 512)),
                  pl.BlockSpec((DEC_SEQ, 512), lambda b, i: (b, COL_DF_V // 512)),
                  pl.BlockSpec((1, 1, 2, DIFF_HEADS, PAST_LEN, DIFF_DH), lambda b, i: (b, l, 0, 0, 0, 0)),
                  pl.BlockSpec((1, 1, DIFF_HEADS, PAST_LEN, 2 * DIFF_DH), lambda b, i: (b, l, 0, 0, 0)),
                  tab, tab, tab,
                  pl.BlockSpec((1, 2 * DIFF_DH), lambda b, i: (0, 0))],
        out_specs=pl.BlockSpec((tq, 512), lambda b, i: (b * nq + i, 0)),
        out_shape=jax.ShapeDtypeStruct((DEC_BATCH * DEC_SEQ, BRANCH_W), F32),
        scratch_shapes=[pltpu.VMEM((DIFF_HEADS, DEC_SEQ + PAST_LEN, LANES), BF16),
                        pltpu.VMEM((DIFF_HEADS, DEC_SEQ + PAST_LEN, 4 * DIFF_DH), BF16)],
        compiler_params=_params("parallel", "arbitrary"),
        name="lat_diff",
    )(lam_p, z, z, z, cache_k, cache_v, *tables, diff_norm.reshape(1, 2 * DIFF_DH))


RET_BLOCK = 256


def _block_diag2(a, b):
    zero = jnp.zeros_like(a)
    return jnp.concatenate([jnp.concatenate([a, zero], axis=1), jnp.concatenate([zero, b], axis=1)], axis=0)


def _ret_kernel(*refs, n_chunks, chained):
    th_ref, q_ref, k_ref, v_ref, g_ref, gn_ref = refs[:6]
    if chained:
        s0f_ref, s0b_ref, y_ref = refs[6:]
    else:
        y_ref, sf_ref, sb_ref = refs[6:]
    C = RET_BLOCK
    dh = RET_DH
    rel = (lax.broadcasted_iota(jnp.int32, (C, C), 0) - lax.broadcasted_iota(jnp.int32, (C, C), 1)).astype(F32)
    pos = lax.broadcasted_iota(jnp.int32, (C, LANES), 0).astype(F32)
    first = lax.broadcasted_iota(jnp.int32, (C, LANES), 1) < dh
    lg = jnp.log1p(-jnp.exp(th_ref[0]))
    lgf, lgb = lg[0:1, :], lg[1:2, :]
    decays = [jnp.exp(jnp.where(rel >= 0, rel * lgf[:, c0:c0 + 1], -rel * lgb[:, c0:c0 + 1])) for c0 in (0, dh)]
    xi_f = jnp.exp((pos + 1.0) * lgf)
    zeta_f = jnp.exp((C - 1.0 - pos) * lgf)
    xi_b = jnp.exp((C - pos) * lgb)
    zeta_b = jnp.exp(pos * lgb)
    gn = gn_ref[...]
    rows = [pl.ds(c * C, C) for c in range(n_chunks)]
    ks = [k_ref[r, :] * (dh ** -0.5) for r in rows]
    vs = [v_ref[r, :] for r in rows]
    uf = [_dot_tn(k * zeta_f, v) for k, v in zip(ks, vs)]
    ub = [_dot_tn(k * zeta_b, v) for k, v in zip(ks, vs)]
    if chained:
        diag = ((lax.broadcasted_iota(jnp.int32, (LANES, LANES), 0) < dh)
                == (lax.broadcasted_iota(jnp.int32, (LANES, LANES), 1) < dh))
        gc_f = jnp.exp(C * lgf)
        gc_b = jnp.exp(C * lgb)
        s_f = [_block_diag2(s0f_ref[0, 0], s0f_ref[0, 1])]
        for c in range(n_chunks - 1):
            s_f.append(gc_f * s_f[c] + jnp.where(diag, uf[c], 0.0))
        s_b = [_block_diag2(s0b_ref[0, 0], s0b_ref[0, 1])]
        for c in range(n_chunks - 1, 0, -1):
            s_b.append(gc_b * s_b[-1] + jnp.where(diag, ub[c], 0.0))
        s_b = s_b[::-1]
    for c in range(n_chunks):
        q = q_ref[rows[c], :]
        kb = ks[c].astype(BF16)
        vb = vs[c].astype(BF16)
        o_heads = [jnp.dot((_dot_nt(jnp.where(keep, q, 0.0), kb) * decay).astype(BF16), vb,
                           preferred_element_type=F32) for keep, decay in zip((first, ~first), decays)]
        o = jnp.where(first, o_heads[0], o_heads[1])
        if chained:
            o = o + _dot(q, s_f[c]) * xi_f + _dot(q, s_b[c]) * xi_b
        else:
            for hh in range(2):
                sf_ref[c, hh] = uf[c][hh * dh:(hh + 1) * dh, hh * dh:(hh + 1) * dh]
                sb_ref[c, hh] = ub[c][hh * dh:(hh + 1) * dh, hh * dh:(hh + 1) * dh]
        sq = o * o
        ms = jnp.where(first, jnp.sum(jnp.where(first, sq, 0.0), axis=-1, keepdims=True),
                       jnp.sum(jnp.where(first, 0.0, sq), axis=-1, keepdims=True)) * (1.0 / dh)
        g = g_ref[rows[c], :]
        y_ref[rows[c], :] = o * lax.rsqrt(ms + EPS) * gn * (g * _sigmoid(g))


def _retention(z, theta, ret_norm, n_batch, t_len, state=None):
    chained = state is not None
    n_chunks = t_len // RET_BLOCK if chained else 4
    rows = n_chunks * RET_BLOCK
    n_steps = n_batch * t_len // rows
    blk = lambda c0: pl.BlockSpec((rows, LANES), lambda b, p: (b, c0 // LANES + p))
    in_specs = [pl.BlockSpec((1, 2, LANES), lambda b, p: (p, 0, 0)),
                blk(COL_RT_Q), blk(COL_RT_K), blk(COL_RT_V), blk(COL_RT_G),
                pl.BlockSpec((1, LANES), lambda b, p: (0, p))]
    args = [theta, z, z, z, z, ret_norm.reshape(1, RET_HEADS * RET_DH)]
    y_spec = pl.BlockSpec((rows, LANES), lambda b, p: (b, p))
    y_shape = jax.ShapeDtypeStruct((n_batch * t_len, BRANCH_W), F32)
    if chained:
        sblk = pl.BlockSpec((1, 2, RET_DH, RET_DH), lambda b, p: (b, p, 0, 0))
        in_specs += [sblk, sblk]
        args += list(state)
        out_specs, out_shape = y_spec, y_shape
    else:
        assert t_len == RET_BLOCK
        sblk = pl.BlockSpec((n_chunks, 2, RET_DH, RET_DH), lambda b, p: (b, p, 0, 0))
        s_shape = jax.ShapeDtypeStruct((n_batch, RET_HEADS, RET_DH, RET_DH), F32)
        out_specs, out_shape = [y_spec, sblk, sblk], [y_shape, s_shape, s_shape]
    return pl.pallas_call(
        functools.partial(_ret_kernel, n_chunks=n_chunks, chained=chained),
        grid=(n_steps, RET_HEADS // 2),
        in_specs=in_specs, out_specs=out_specs, out_shape=out_shape,
        compiler_params=_params("parallel", "parallel"),
        name="retention_lat" if chained else "retention_ctx",
    )(*args)


def _shift_rows(x, k, fill, up):
    t_len = x.shape[0]
    if k % SUBLANES == 0:
        pad = jnp.full((k, x.shape[1]), fill, x.dtype)
        return jnp.concatenate([x[k:], pad], axis=0) if up else jnp.concatenate([pad, x[:t_len - k]], axis=0)
    row = lax.broadcasted_iota(jnp.int32, x.shape, 0)
    if up:
        return jnp.where(row < t_len - k, pltpu.roll(x, t_len - k, axis=0), fill)
    return jnp.where(row >= k, pltpu.roll(x, k, axis=0), fill)


def _linear_scan(a, u, up):
    t_len = a.shape[0]
    k = 1
    while k < t_len:
        u = a * _shift_rows(u, k, 0.0, up) + u
        if 2 * k < t_len:
            a = a * _shift_rows(a, k, 1.0, up)
        k *= 2
    return u


def _lru_kernel(*refs, has_state, write_state):
    x_ref, g_ref, vec_ref, w_ref = refs[:4]
    refs = refs[4:]
    if has_state:
        h0f_ref, h0b_ref = refs[:2]
        refs = refs[2:]
    y_ref = refs[0]
    if write_state:
        hf_ref, hb_ref = refs[1:3]
    x = x_ref[...]
    t_len = x.shape[0]
    row = lax.broadcasted_iota(jnp.int32, x.shape, 0)
    vec = lambda i: vec_ref[i:i + 1, :]
    xd = (vec(0) * _shift_rows(x, 1, 0.0, False) + vec(1) * x + vec(2) * _shift_rows(x, 1, 0.0, True)
          + vec(3) * _shift_rows(x, 2, 0.0, True) + vec(4))

    def gates(wa, wx, ba, bx, lam):
        r = _sigmoid(_dot(xd, wa) + ba)
        i = _sigmoid(_dot(xd, wx) + bx)
        nl = -lam
        softplus = jnp.maximum(nl, 0.0) + jnp.log1p(jnp.exp(-jnp.abs(nl)))
        log_a = -LRU_C * r * softplus
        a = jnp.exp(log_a)
        return a, jnp.sqrt(-jnp.tanh(log_a) * (a * a + 1.0)) * (i * xd)

    a, u = gates(w_ref[0, 0], w_ref[1, 0], vec(5), vec(6), vec(7))
    if has_state:
        u = u + jnp.where(row == 0, a * h0f_ref[0], 0.0)
    h_f = _linear_scan(a, u, False)
    a, u = gates(w_ref[2, 0], w_ref[3, 0], vec(8), vec(9), vec(10))
    if has_state:
        u = u + jnp.where(row == t_len - 1, a * h0b_ref[0], 0.0)
    h_b = _linear_scan(a, u, True)
    g = g_ref[...]
    gelu = 0.5 * g * (1.0 + jnp.tanh(math.sqrt(2.0 / math.pi) * (g + 0.044715 * (g * g * g))))
    y_ref[...] = (h_f + h_b) * gelu
    if write_state:
        hf_ref[0] = h_f[t_len - 1:t_len, :]
        hb_ref[0] = h_b[0:1, :]


def _rglru(z, vecs, w_gate, n_batch, t_len, state=None):
    blk = lambda c0: pl.BlockSpec((t_len, LANES), lambda b, j: (b, c0 // LANES + j))
    hblk = pl.BlockSpec((1, 1, LANES), lambda b, j: (b, 0, j))
    in_specs = [blk(COL_LR_X), blk(COL_LR_G),
                pl.BlockSpec((11, LANES), lambda b, j: (0, j)),
                pl.BlockSpec((4, 1, LANES, LANES), lambda b, j: (0, j, 0, 0))]
    args = [z, z, vecs, w_gate]
    y_spec = pl.BlockSpec((t_len, LANES), lambda b, j: (b, j))
    y_shape = jax.ShapeDtypeStruct((n_batch * t_len, BRANCH_W), F32)
    if state is not None:
        in_specs += [hblk, hblk]
        args += [s.reshape(n_batch, 1, LRU_WIDTH) for s in state]
        out_specs, out_shape = y_spec, y_shape
    else:
        h_shape = jax.ShapeDtypeStruct((n_batch, 1, LRU_WIDTH), F32)
        out_specs, out_shape = [y_spec, hblk, hblk], [y_shape, h_shape, h_shape]
    return pl.pallas_call(
        functools.partial(_lru_kernel, has_state=state is not None, write_state=state is None),
        grid=(n_batch, LRU_WIDTH // LANES),
        in_specs=in_specs, out_specs=out_specs, out_shape=out_shape,
        compiler_params=_params("parallel", "parallel"),
        name="rglru_lat" if state is not None else "rglru_ctx",
    )(*args)


def _pair_block_diag(w):
    bw = LRU_WIDTH // LRU_BLOCKS
    w = w.reshape(LRU_BLOCKS // 2, 2, bw, bw)
    zero = jnp.zeros_like(w[:, 0])
    top = jnp.concatenate([w[:, 0], zero], axis=2)
    bot = jnp.concatenate([zero, w[:, 1]], axis=2)
    return jnp.concatenate([top, bot], axis=1)


def _merge_kernel(x_ref, ya_ref, yb_ref, yc_ref, yd_ref, wg_ref, wb_ref, wo_ref, gpre_ref, gpost_ref, m_ref, o_ref):
    x = x_ref[...]
    h = (_rms(x, gpre_ref[...]) * (1.0 + m_ref[0, 1:2, :]) + m_ref[0, 0:1, :]).astype(BF16)
    acc = None
    for k, y_ref in enumerate((ya_ref, yb_ref, yc_ref, yd_ref)):
        logits = jnp.dot(h, wg_ref[:, k * D_MODEL:(k + 1) * D_MODEL], preferred_element_type=F32)
        term = _sigmoid(logits) * jnp.dot(y_ref[...].astype(BF16), wb_ref[k], preferred_element_type=F32)
        acc = term if acc is None else acc + term
    out = jnp.dot(acc.astype(BF16), wo_ref[...], preferred_element_type=F32)
    o_ref[...] = x + m_ref[0, 2:3, :] * _rms(out, gpost_ref[...])


def _merge(x, ys, mod, wg_bf, wb_bf, wo_bf, gpre, gpost, mod_row):
    n_tok = x.shape[0]
    tm = 256
    yblk = pl.BlockSpec((tm, BRANCH_W), lambda i: (i, 0))
    return pl.pallas_call(
        _merge_kernel,
        grid=(n_tok // tm,),
        in_specs=[pl.BlockSpec((tm, D_MODEL), lambda i: (i, 0)), yblk, yblk, yblk, yblk,
                  pl.BlockSpec((D_MODEL, N_BRANCH * D_MODEL), lambda i: (0, 0)),
                  pl.BlockSpec((N_BRANCH, BRANCH_W, D_MODEL), lambda i: (0, 0, 0)),
                  pl.BlockSpec((D_MODEL, D_MODEL), lambda i: (0, 0)),
                  pl.BlockSpec((1, D_MODEL), lambda i: (0, 0)),
                  pl.BlockSpec((1, D_MODEL), lambda i: (0, 0)),
                  pl.BlockSpec((1, 6, D_MODEL), lambda i: (mod_row(i, tm), 0, 0))],
        out_specs=pl.BlockSpec((tm, D_MODEL), lambda i: (i, 0)),
        out_shape=jax.ShapeDtypeStruct((n_tok, D_MODEL), F32),
        compiler_params=_params("parallel"),
        name="merge",
    )(x, *ys, wg_bf, wb_bf, wo_bf, gpre.reshape(1, D_MODEL), gpost.reshape(1, D_MODEL), mod)


def _mlp_kernel(x_ref, m_ref, gpre_ref, gpost_ref, w1_ref, w2_ref, o_ref):
    x = x_ref[...]
    h = (_rms(x, gpre_ref[...]) * (1.0 + m_ref[0, 4:5, :]) + m_ref[0, 3:4, :]).astype(BF16)
    ff_chunk = D_MODEL
    y = None
    for j in range(D_FF // ff_chunk):
        sl = slice(j * ff_chunk, (j + 1) * ff_chunk)
        a = jnp.maximum(jnp.dot(h, w1_ref[:, sl], preferred_element_type=F32), 0.0)
        part = jnp.dot((a * a).astype(BF16), w2_ref[sl, :], preferred_element_type=F32)
        y = part if y is None else y + part
    o_ref[...] = x + m_ref[0, 5:6, :] * _rms(y, gpost_ref[...])


def _mlp(x, mod, gpre, gpost, w1_bf, w2_bf, mod_row):
    n_tok = x.shape[0]
    tm = 256
    return pl.pallas_call(
        _mlp_kernel,
        grid=(n_tok // tm,),
        in_specs=[pl.BlockSpec((tm, D_MODEL), lambda i: (i, 0)),
                  pl.BlockSpec((1, 6, D_MODEL), lambda i: (mod_row(i, tm), 0, 0)),
                  pl.BlockSpec((1, D_MODEL), lambda i: (0, 0)),
                  pl.BlockSpec((1, D_MODEL), lambda i: (0, 0)),
                  pl.BlockSpec((D_MODEL, D_FF), lambda i: (0, 0)),
                  pl.BlockSpec((D_FF, D_MODEL), lambda i: (0, 0))],
        out_specs=pl.BlockSpec((tm, D_MODEL), lambda i: (i, 0)),
        out_shape=jax.ShapeDtypeStruct((n_tok, D_MODEL), F32),
        compiler_params=_params("parallel"),
        name="mlp",
    )(x, mod, gpre.reshape(1, D_MODEL), gpost.reshape(1, D_MODEL), w1_bf, w2_bf)


def _cache_heads_kernel(*refs, n_heads, dh):
    o_ref = refs[-1]
    for l, z_ref in enumerate(refs[:-1]):
        for h in range(n_heads):
            o_ref[0, l, h] = z_ref[:, h * dh:(h + 1) * dh]


def _cache_heads(zs, col, n_heads, dh):
    return pl.pallas_call(
        functools.partial(_cache_heads_kernel, n_heads=n_heads, dh=dh),
        grid=(BATCH,),
        in_specs=[pl.BlockSpec((SEQ, 512), lambda b: (b, col // 512))] * DEPTH,
        out_specs=pl.BlockSpec((1, DEPTH, n_heads, SEQ, dh), lambda b: (b, 0, 0, 0, 0)),
        out_shape=jax.ShapeDtypeStruct((BATCH, DEPTH, n_heads, SEQ, dh), F32),
        compiler_params=_params("parallel"),
        name="cache_heads",
    )(*zs)


def _ctx_mod_row(i, tm):
    return 0


def _lat_mod_row(i, tm):
    return 1 + i // (DEC_SEQ // tm)


def kernel(x_prompt, x_sample, cache_na_k, cache_na_v, cache_diff_k, cache_diff_v, state_ret_fwd, state_ret_bwd, state_lru_fwd, state_lru_bwd, c, c_ctx, ada_w, ada_b, norm_mix_pre, norm_mix_post, norm_ffn_pre, norm_ffn_post, w_in, na_rpb, diff_lq1, diff_lk1, diff_lq2, diff_lk2, diff_norm, ret_theta_fwd, ret_theta_bwd, ret_norm, lru_conv_w, lru_conv_b, lru_wa_fwd, lru_ba_fwd, lru_wx_fwd, lru_bx_fwd, lru_lam_fwd, lru_wa_bwd, lru_ba_bwd, lru_wx_bwd, lru_bx_bwd, lru_lam_bwd, w_branch, w_out, mlp_w1, mlp_w2):
    xc = x_prompt.reshape(BATCH * SEQ, D_MODEL)
    xl = x_sample.reshape(DEC_BATCH * DEC_SEQ, D_MODEL)
    cond = jnp.concatenate([c_ctx[None, :], c, jnp.zeros((COND_PAD - N_COND, D_MODEL), F32)], axis=0)
    mods = _modulation(cond.T, ada_w, ada_b).reshape(DEPTH, COND_PAD, 6, D_MODEL)
    rope_tables = _rope_tables()
    pair_tables = _na_pair_tables(_na_col_tables(na_rpb))
    outs = [[] for _ in range(4)]
    zcs = []
    for l in range(DEPTH):
        mod = mods[l]
        lam_init = 0.8 - 0.6 * math.exp(-0.3 * l)
        w_in_bf = w_in[l, :, :Z_WIDTH].astype(BF16)
        wg_bf = w_in[l, :, Z_WIDTH:].astype(BF16)
        wb_bf = w_branch[l].astype(BF16)
        wo_bf = w_out[l].astype(BF16)
        w1_bf = mlp_w1[l].astype(BF16)
        w2_bf = mlp_w2[l].astype(BF16)
        lam_p = jnp.stack([diff_lq1[l], diff_lk1[l], diff_lq2[l], diff_lk2[l]])
        theta = jnp.stack([jnp.repeat(t, RET_DH).reshape(RET_HEADS // 2, LANES)
                           for t in (ret_theta_fwd[l], ret_theta_bwd[l])], axis=1)
        vecs = jnp.concatenate([lru_conv_w[l], lru_conv_b[l][None],
                                lru_ba_fwd[l][None], lru_bx_fwd[l][None], lru_lam_fwd[l][None],
                                lru_ba_bwd[l][None], lru_bx_bwd[l][None], lru_lam_bwd[l][None]], axis=0)
        w_gate = jnp.stack([_pair_block_diag(w) for w in
                            (lru_wa_fwd[l], lru_wx_fwd[l], lru_wa_bwd[l], lru_wx_bwd[l])])

        zc = _inproj(xc, mod, norm_mix_pre[l], w_in_bf, _ctx_mod_row)
        ya = _ctx_na(zc)
        yb = _ctx_diff(zc, lam_p, diff_norm[l], lam_init)
        yc, s_rf, s_rb = _retention(zc, theta, ret_norm[l], BATCH, SEQ)
        yd, h_lf, h_lb = _rglru(zc, vecs, w_gate, BATCH, SEQ)
        xc = _merge(xc, (ya, yb, yc, yd), mod, wg_bf, wb_bf, wo_bf, norm_mix_pre[l], norm_mix_post[l], _ctx_mod_row)
        xc = _mlp(xc, mod, norm_ffn_pre[l], norm_ffn_post[l], w1_bf, w2_bf, _ctx_mod_row)
        zcs.append(zc)
        outs[0].append(s_rf)
        outs[1].append(s_rb)
        outs[2].append(h_lf.reshape(BATCH, LRU_WIDTH))
        outs[3].append(h_lb.reshape(BATCH, LRU_WIDTH))

        zl = _inproj(xl, mod, norm_mix_pre[l], w_in_bf, _lat_mod_row)
        ya = _lat_na(zl, cache_na_k, cache_na_v, pair_tables, l)
        yb = _lat_diff(zl, cache_diff_k, cache_diff_v, lam_p, diff_norm[l], rope_tables, lam_init, l)
        yc = _retention(zl, theta, ret_norm[l], DEC_BATCH, DEC_SEQ,
                        state=(state_ret_fwd[:, l], state_ret_bwd[:, l]))
        yd = _rglru(zl, vecs, w_gate, DEC_BATCH, DEC_SEQ, state=(state_lru_fwd[:, l], state_lru_bwd[:, l]))
        xl = _merge(xl, (ya, yb, yc, yd), mod, wg_bf, wb_bf, wo_bf, norm_mix_pre[l], norm_mix_post[l], _lat_mod_row)
        xl = _mlp(xl, mod, norm_ffn_pre[l], norm_ffn_post[l], w1_bf, w2_bf, _lat_mod_row)

    new_na_k = _cache_heads(zcs, COL_NA_K, NA_HEADS, NA_DH)
    new_na_v = _cache_heads(zcs, COL_NA_V, NA_HEADS, NA_DH)
    new_diff_k = _cache_heads(zcs, COL_DF_K, 2 * DIFF_HEADS, DIFF_DH).reshape(
        BATCH, DEPTH, 2, DIFF_HEADS, SEQ, DIFF_DH)
    new_diff_v = _cache_heads(zcs, COL_DF_V, DIFF_HEADS, 2 * DIFF_DH)
    states = [jnp.stack(o, axis=1) for o in outs]
    return (xc.reshape(BATCH, SEQ, D_MODEL), xl.reshape(DEC_BATCH, DEC_SEQ, D_MODEL),
            new_na_k, new_na_v, new_diff_k, new_diff_v, *states)
```

```python
import functools
import math

import numpy as np
import jax
import jax.numpy as jnp
from jax import lax
from jax.experimental import pallas as pl
from jax.experimental.pallas import tpu as pltpu

F32 = jnp.float32
BF16 = jnp.bfloat16

D_MODEL = 1024
BATCH = 32
SEQ = 256
DEPTH = 4
DEC_BATCH = 4
DEC_SEQ = 2048
PAST_LEN = 512
GRID_W = 64
NA_HEADS = 8
NA_DH = 64
NA_ROWS = 8
NA_COLS = 16
DIFF_HEADS = 4
DIFF_DH = 64
RET_HEADS = 8
RET_DH = 64
LRU_WIDTH = 512
LRU_BLOCKS = 8
LRU_CONV = 4
LRU_C = 8.0
N_BRANCH = 4
BRANCH_W = 512
D_FF = 4 * D_MODEL
ROPE_BASE = 10000.0
EPS = 1e-6
NEG_INF = -1e30

COL_NA_Q, COL_NA_K, COL_NA_V = 0, 512, 1024
COL_DF_Q, COL_DF_K, COL_DF_V = 1536, 2048, 2560
COL_RT_Q, COL_RT_K, COL_RT_V, COL_RT_G = 3072, 3584, 4096, 4608
COL_LR_X, COL_LR_G = 5120, 5632
COL_GATE = 6144
Z_WIDTH = COL_GATE

N_COND = 1 + DEC_BATCH
COND_PAD = 8
LANES = 128
SUBLANES = 8
VMEM_LIMIT = 56 * 1024 * 1024


def _params(*sem):
    return pltpu.CompilerParams(dimension_semantics=sem, vmem_limit_bytes=VMEM_LIMIT)


def _dot(a, b):
    return jnp.dot(a.astype(BF16), b.astype(BF16), preferred_element_type=F32)


def _dot_nt(a, b):
    return lax.dot_general(a.astype(BF16), b.astype(BF16), (((1,), (1,)), ((), ())),
                           preferred_element_type=F32)


def _dot_tn(a, b):
    return lax.dot_general(a.astype(BF16), b.astype(BF16), (((0,), (0,)), ((), ())),
                           preferred_element_type=F32)


def _rms(x, g):
    return x * lax.rsqrt(jnp.mean(x * x, axis=-1, keepdims=True) + EPS) * g


def _sigmoid(x):
    return 1.0 / (1.0 + jnp.exp(-x))


LOG2E = math.log2(math.e)


def _exp2_weights(parts):
    m = functools.reduce(jnp.maximum, [jnp.max(s, axis=-1, keepdims=True) for s in parts])
    return [jnp.exp2(s - m).astype(BF16) for s in parts]


def _with_ones(v):
    return jnp.concatenate([v.astype(BF16), jnp.ones(v.shape, BF16)], axis=1)


def _normalised(r):
    w = r.shape[1] // 2
    return r[:, :w] / r[:, w:]


def _mod_kernel(ct_ref, w_ref, b_ref, o_ref):
    ct = ct_ref[...]
    s = ct * _sigmoid(ct)
    w = w_ref[0]
    rows = [jnp.sum(s[:, r:r + 1] * w, axis=0, keepdims=True) + b_ref[0] for r in range(N_COND)]
    rows.append(jnp.zeros((COND_PAD - N_COND, w.shape[1]), F32))
    o_ref[0] = jnp.concatenate(rows, axis=0)


def _modulation(cond_t, ada_w, ada_b):
    tn = 768
    n = 6 * D_MODEL
    return pl.pallas_call(
        _mod_kernel,
        grid=(DEPTH, n // tn),
        in_specs=[pl.BlockSpec((D_MODEL, COND_PAD), lambda l, j: (0, 0)),
                  pl.BlockSpec((1, D_MODEL, tn), lambda l, j: (l, 0, j)),
                  pl.BlockSpec((1, 1, tn), lambda l, j: (l, 0, j))],
        out_specs=pl.BlockSpec((1, COND_PAD, tn), lambda l, j: (l, 0, j)),
        out_shape=jax.ShapeDtypeStruct((DEPTH, COND_PAD, n), F32),
        compiler_params=_params("parallel", "parallel"),
        name="modulation",
    )(cond_t, ada_w, ada_b.reshape(DEPTH, 1, n))


def _inproj_kernel(x_ref, m_ref, g_ref, w_ref, z_ref, h_scr):
    @pl.when(pl.program_id(1) == 0)
    def _():
        h = _rms(x_ref[...], g_ref[...]) * (1.0 + m_ref[0, 1:2, :]) + m_ref[0, 0:1, :]
        h_scr[...] = h.astype(BF16)

    z_ref[...] = jnp.dot(h_scr[...], w_ref[0], preferred_element_type=F32)


def _inproj(x, mod, gain, w_bf, l, mod_row):
    n_tok = x.shape[0]
    tm, tn = 2048, 1024
    return pl.pallas_call(
        _inproj_kernel,
        grid=(n_tok // tm, Z_WIDTH // tn),
        in_specs=[pl.BlockSpec((tm, D_MODEL), lambda i, j: (i, 0)),
                  pl.BlockSpec((1, 6, D_MODEL), lambda i, j: (mod_row(i, tm), 0, 0)),
                  pl.BlockSpec((1, D_MODEL), lambda i, j: (0, 0)),
                  pl.BlockSpec((1, D_MODEL, tn), lambda i, j: (l, 0, j))],
        out_specs=pl.BlockSpec((tm, tn), lambda i, j: (i, j)),
        out_shape=jax.ShapeDtypeStruct((n_tok, Z_WIDTH), F32),
        scratch_shapes=[pltpu.VMEM((tm, D_MODEL), BF16)],
        compiler_params=_params("parallel", "arbitrary"),
        name="inproj",
    )(x, mod, gain.reshape(1, D_MODEL), w_bf)


def _ctx_na_kernel(q_ref, k_ref, v_ref, o_ref):
    qs = NA_DH ** -0.5 * LOG2E
    first = lax.broadcasted_iota(jnp.int32, (SEQ, LANES), 1) < NA_DH
    for p in range(NA_HEADS * NA_DH // LANES):
        sl = slice(p * LANES, (p + 1) * LANES)
        q = q_ref[:, sl] * qs
        k = k_ref[:, sl].astype(BF16)
        vx = _with_ones(v_ref[:, sl])
        att = []
        for keep in (first, ~first):
            (e,) = _exp2_weights([_dot_nt(jnp.where(keep, q, 0.0), k)])
            att.append(_normalised(jnp.dot(e, vx, preferred_element_type=F32)))
        o_ref[:, sl] = jnp.where(first, att[0], att[1])


def _ctx_na(z):
    blk = lambda j: pl.BlockSpec((SEQ, 512), lambda b: (b, j))
    return pl.pallas_call(
        _ctx_na_kernel,
        grid=(BATCH,),
        in_specs=[blk(COL_NA_Q // 512), blk(COL_NA_K // 512), blk(COL_NA_V // 512)],
        out_specs=pl.BlockSpec((SEQ, 512), lambda b: (b, 0)),
        out_shape=jax.ShapeDtypeStruct((BATCH * SEQ, BRANCH_W), F32),
        compiler_params=_params("parallel"),
        name="ctx_na",
    )(z, z, z)


def _diff_lambda(lam_ref, lam_init):
    a = jnp.sum(lam_ref[0:1, :] * lam_ref[1:2, :], axis=-1, keepdims=True)
    b = jnp.sum(lam_ref[2:3, :] * lam_ref[3:4, :], axis=-1, keepdims=True)
    return jnp.exp(a) - jnp.exp(b) + lam_init


def _ctx_diff_kernel(lam_ref, q_ref, k_ref, v_ref, dn_ref, o_ref, *, lam_init):
    qs = DIFF_DH ** -0.5 * LOG2E
    lam = _diff_lambda(lam_ref, lam_init)
    w = 2 * DIFF_DH
    first = lax.broadcasted_iota(jnp.int32, (SEQ, LANES), 1) < DIFF_DH
    for h in range(DIFF_HEADS):
        vx = _with_ones(v_ref[:, h * w:(h + 1) * w])
        keep = first if h % 2 == 0 else ~first
        att = []
        for c in range(2):
            c0 = (c * DIFF_HEADS + h - h % 2) * DIFF_DH
            sl = slice(c0, c0 + LANES)
            (e,) = _exp2_weights([_dot_nt(jnp.where(keep, q_ref[:, sl] * qs, 0.0), k_ref[:, sl])])
            att.append(_normalised(jnp.dot(e, vx, preferred_element_type=F32)))
        o = att[0] - lam * att[1]
        o_ref[:, h * w:(h + 1) * w] = _rms(o, dn_ref[...]) * (1.0 - lam_init)


def _ctx_diff(z, lam_p, diff_norm, lam_init):
    blk = lambda j: pl.BlockSpec((SEQ, 512), lambda b: (b, j))
    return pl.pallas_call(
        functools.partial(_ctx_diff_kernel, lam_init=lam_init),
        grid=(BATCH,),
        in_specs=[pl.BlockSpec((4, DIFF_DH), lambda b: (0, 0)),
                  blk(COL_DF_Q // 512), blk(COL_DF_K // 512), blk(COL_DF_V // 512),
                  pl.BlockSpec((1, 2 * DIFF_DH), lambda b: (0, 0))],
        out_specs=pl.BlockSpec((SEQ, 512), lambda b: (b, 0)),
        out_shape=jax.ShapeDtypeStruct((BATCH * SEQ, BRANCH_W), F32),
        compiler_params=_params("parallel"),
        name="ctx_diff",
    )(lam_p, z, z, z, diff_norm.reshape(1, 2 * DIFF_DH))


NA_GRID_ROWS = DEC_SEQ // GRID_W
NA_KEY_ROWS = min(NA_ROWS, NA_GRID_ROWS)


NA_DR = 2 * NA_ROWS - 1
NA_DC = 2 * NA_COLS - 1


def _na_col_kernel(rpb_ref, idx_ref, o_ref):
    idx = jnp.broadcast_to(idx_ref[...], o_ref.shape)
    acc = jnp.full(o_ref.shape, NEG_INF, F32)
    for d in range(NA_DC):
        acc = jnp.where(idx == d, jnp.broadcast_to(rpb_ref[:, d:d + 1], o_ref.shape), acc)
    o_ref[...] = acc * LOG2E


def _na_col_tables(na_rpb):
    qc = np.arange(GRID_W)[:, None]
    kc = np.arange(GRID_W)[None, :]
    cstart = np.clip(qc - NA_COLS // 2, 0, GRID_W - NA_COLS)
    ok = (kc >= cstart) & (kc < cstart + NA_COLS)
    idx = np.where(ok, kc - qc + NA_COLS - 1, -1).astype(np.int32).reshape(1, GRID_W * GRID_W)
    rows = NA_HEADS * NA_DR
    rpb = jnp.pad(na_rpb.reshape(DEPTH * rows, NA_DC), ((0, 0), (0, 32 - NA_DC)))
    t = pl.pallas_call(
        _na_col_kernel,
        grid=(DEPTH,),
        in_specs=[pl.BlockSpec((rows, 32), lambda l: (l, 0)),
                  pl.BlockSpec((1, GRID_W * GRID_W), lambda l: (0, 0))],
        out_specs=pl.BlockSpec((rows, GRID_W * GRID_W), lambda l: (l, 0)),
        out_shape=jax.ShapeDtypeStruct((DEPTH * rows, GRID_W * GRID_W), F32),
        compiler_params=_params("parallel"),
        name="na_col_tables",
    )(rpb, jnp.asarray(idx))
    return t.reshape(DEPTH, NA_HEADS, NA_DR, GRID_W, GRID_W)


NA_QROWS = 4
NA_UROWS = 12
NA_QBLK = NA_QROWS * GRID_W
NA_UKEYS = NA_UROWS * GRID_W
NA_NBLK = NA_GRID_ROWS // NA_QROWS


def _na_union_start(r0):
    return int(np.clip(r0 - NA_KEY_ROWS // 2, 0, NA_GRID_ROWS - NA_UROWS))


def _na_pair_tables(t):
    tp = jnp.pad(t, ((0, 0), (0, 0), (1, 1), (0, 0), (0, 0)), constant_values=NEG_INF)
    return jnp.concatenate([tp[:, :, :-1], tp[:, :, 1:]], axis=-1)


def _na_fill_bias(pair_ref, bias_scr):
    first = lax.broadcasted_iota(jnp.int32, (GRID_W, LANES), 1) < GRID_W
    neg = jnp.full((GRID_W, LANES), NEG_INF, F32)
    for variant, r0 in enumerate((0, NA_QROWS, NA_GRID_ROWS - NA_QROWS)):
        us = _na_union_start(r0)
        for j in range(NA_QROWS):
            r = r0 + j
            rs = int(np.clip(r - NA_KEY_ROWS // 2, 0, NA_GRID_ROWS - NA_KEY_ROWS))
            for m in range(NA_UROWS // 2):
                ok = [rs <= us + i < rs + NA_KEY_ROWS for i in (2 * m, 2 * m + 1)]
                p = us + 2 * m - r + NA_ROWS
                for hh in range(2):
                    if ok[0] and ok[1]:
                        tile = pair_ref[0, hh, p]
                    elif ok[0]:
                        tile = jnp.where(first, pair_ref[0, hh, p], NEG_INF)
                    elif ok[1]:
                        tile = jnp.where(first, NEG_INF, pair_ref[0, hh, p])
                    else:
                        tile = neg
                    bias_scr[variant, hh, j * GRID_W:(j + 1) * GRID_W, m * LANES:(m + 1) * LANES] = tile


def _lat_na_kernel(q_ref, k_ref, v_ref, kc_ref, vc_ref, pair_ref, o_ref, k_scr, vx_scr, kc_scr, vcx_scr, bias_ref):
    qs = NA_DH ** -0.5 * LOG2E
    _na_fill_bias(pair_ref, bias_ref)
    k_scr[...] = k_ref[...].astype(BF16)
    vx_scr[...] = _with_ones(v_ref[...])
    kc_scr[...] = jnp.concatenate([kc_ref[0, 0, 0], kc_ref[0, 0, 1]], axis=0).astype(BF16)
    vcx_scr[...] = jnp.concatenate([vc_ref[0, 0, 0].astype(BF16), vc_ref[0, 0, 1].astype(BF16),
                                    jnp.ones((LANES, PAST_LEN), BF16)], axis=0)
    first = lax.broadcasted_iota(jnp.int32, (NA_QBLK, LANES), 1) < NA_DH

    for blk in range(NA_NBLK):
        us = _na_union_start(blk * NA_QROWS)
        variant = 0 if blk == 0 else (2 if blk == NA_NBLK - 1 else 1)
        rows = slice(blk * NA_QBLK, (blk + 1) * NA_QBLK)
        keys = slice(us * GRID_W, us * GRID_W + NA_UKEYS)
        q = q_ref[rows, :] * qs
        att = []
        for hh, keep in enumerate((first, ~first)):
            qm = jnp.where(keep, q, 0.0).astype(BF16)
            s_loc = _dot_nt(qm, k_scr[keys, :]) + bias_ref[variant, hh]
            s_ctx = jnp.dot(qm, kc_scr[...], preferred_element_type=F32)
            e_loc, e_ctx = _exp2_weights([s_loc, s_ctx])
            att.append(_normalised(jnp.dot(e_loc, vx_scr[keys, :], preferred_element_type=F32)
                                   + _dot_nt(e_ctx, vcx_scr[...])))
        o_ref[rows, :] = jnp.where(first, att[0], att[1])


def _lat_na(z, cache_k, cache_v, pair_tables, l):
    blk = lambda c0: pl.BlockSpec((DEC_SEQ, LANES), lambda p, b: (b, c0 // LANES + p))
    cblk = pl.BlockSpec((1, 1, 2, NA_DH, PAST_LEN), lambda p, b: (b, l, p, 0, 0))
    return pl.pallas_call(
        _lat_na_kernel,
        grid=(NA_HEADS // 2, DEC_BATCH),
        in_specs=[blk(COL_NA_Q), blk(COL_NA_K), blk(COL_NA_V), cblk, cblk,
                  pl.BlockSpec((1, 2, NA_DR + 1, GRID_W, LANES), lambda p, b: (l, p, 0, 0, 0))],
        out_specs=pl.BlockSpec((DEC_SEQ, LANES), lambda p, b: (b, p)),
        out_shape=jax.ShapeDtypeStruct((DEC_BATCH * DEC_SEQ, BRANCH_W), F32),
        scratch_shapes=[pltpu.VMEM((DEC_SEQ, LANES), BF16), pltpu.VMEM((DEC_SEQ, 2 * LANES), BF16),
                        pltpu.VMEM((LANES, PAST_LEN), BF16), pltpu.VMEM((2 * LANES, PAST_LEN), BF16),
                        pltpu.VMEM((3, 2, NA_QBLK, NA_UKEYS), F32)],
        compiler_params=_params("parallel", "parallel"),
        name="lat_na",
    )(z, z, z, cache_k, cache_v, pair_tables)


def _rope_tables():
    t = jnp.arange(DEC_SEQ)
    quarter = DIFF_DH // 4
    inv = ROPE_BASE ** (-jnp.arange(quarter, dtype=F32) / quarter)
    ang_r = (t // GRID_W).astype(F32)[:, None] * inv
    ang_c = (t % GRID_W).astype(F32)[:, None] * inv
    lane = np.arange(LANES) % DIFF_DH
    idx = lane % quarter
    use_r = jnp.asarray(lane < DIFF_DH // 2)[None, :]
    first = jnp.asarray((lane % (2 * quarter)) < quarter)[None, :]
    ang = jnp.where(use_r, ang_r[:, idx], ang_c[:, idx])
    cos = jnp.cos(ang)
    sin = jnp.sin(ang)
    return cos, jnp.where(first, -sin, 0.0), jnp.where(first, 0.0, sin)


def _rope(x, cos, sin_a, sin_b):
    quarter = DIFF_DH // 4
    up = pltpu.roll(x, LANES - quarter, axis=1)
    down = pltpu.roll(x, quarter, axis=1)
    return x * cos + up * sin_a + down * sin_b


def _lat_diff_kernel(lam_ref, q_ref, k_ref, v_ref, kc_ref, vc_ref, cos_ref, sa_ref, sb_ref, dn_ref, o_ref,
                     k_scr, vx_scr, *, lam_init, tq):
    qs = DIFF_DH ** -0.5 * LOG2E
    n_slab = 512 // LANES
    w = 2 * DIFF_DH
    lat = slice(0, DEC_SEQ)
    ctx = slice(DEC_SEQ, DEC_SEQ + PAST_LEN)

    @pl.when(pl.program_id(1) == 0)
    def _():
        for j in range(n_slab):
            k_scr[j, lat, :] = _rope(k_ref[:, j * LANES:(j + 1) * LANES],
                                     cos_ref[...], sa_ref[...], sb_ref[...]).astype(BF16)
        for c in range(2):
            for hp in range(DIFF_HEADS // 2):
                k_scr[c * (DIFF_HEADS // 2) + hp, ctx, :] = jnp.concatenate(
                    [kc_ref[0, 0, c, 2 * hp], kc_ref[0, 0, c, 2 * hp + 1]], axis=0).T.astype(BF16)
        for h in range(DIFF_HEADS):
            vx_scr[h, lat, :] = _with_ones(v_ref[:, h * w:(h + 1) * w])
            vx_scr[h, ctx, :] = _with_ones(vc_ref[0, 0, h])

    r0 = pl.multiple_of(pl.program_id(1) * tq, tq)
    cos, sa, sb = cos_ref[pl.ds(r0, tq), :], sa_ref[pl.ds(r0, tq), :], sb_ref[pl.ds(r0, tq), :]
    qrot = [_rope(q_ref[:, j * LANES:(j + 1) * LANES], cos, sa, sb) * qs for j in range(n_slab)]
    lam = _diff_lambda(lam_ref, lam_init)
    first = lax.broadcasted_iota(jnp.int32, (tq, LANES), 1) < DIFF_DH
    for h in range(DIFF_HEADS):
        keep = first if h % 2 == 0 else ~first
        att = []
        for c in range(2):
            j = c * (DIFF_HEADS // 2) + h // 2
            (e,) = _exp2_weights([_dot_nt(jnp.where(keep, qrot[j], 0.0), k_scr[j])])
            att.append(_normalised(jnp.dot(e, vx_scr[h], preferred_element_type=F32)))
        o = att[0] - lam * att[1]
        o_ref[:, h * w:(h + 1) * w] = _rms(o, dn_ref[...]) * (1.0 - lam_init)


def _lat_diff(z, cache_k, cache_v, lam_p, diff_norm, tables, lam_init, l):
    tq = 256
    nq = DEC_SEQ // tq
    tab = pl.BlockSpec((DEC_SEQ, LANES), lambda b, i: (0, 0))
    return pl.pallas_call(
        functools.partial(_lat_diff_kernel, lam_init=lam_init, tq=tq),
        grid=(DEC_BATCH, nq),
        in_specs=[pl.BlockSpec((4, DIFF_DH), lambda b, i: (0, 0)),
                  pl.BlockSpec((tq, 512), lambda b, i: (b * nq + i, COL_DF_Q // 512)),
                  pl.BlockSpec((DEC_SEQ, 512), lambda b, i: (b, COL_DF_K // 512)),
                  pl.BlockSpec((DEC_SEQ, 512), lambda b, i: (b, COL_DF_V // 512)),
                  pl.BlockSpec((1, 1, 2, DIFF_HEADS, DIFF_DH, PAST_LEN), lambda b, i: (b, l, 0, 0, 0, 0)),
                  pl.BlockSpec((1, 1, DIFF_HEADS, PAST_LEN, 2 * DIFF_DH), lambda b, i: (b, l, 0, 0, 0)),
                  tab, tab, tab,
                  pl.BlockSpec((1, 2 * DIFF_DH), lambda b, i: (0, 0))],
        out_specs=pl.BlockSpec((tq, 512), lambda b, i: (b * nq + i, 0)),
        out_shape=jax.ShapeDtypeStruct((DEC_BATCH * DEC_SEQ, BRANCH_W), F32),
        scratch_shapes=[pltpu.VMEM((DIFF_HEADS, DEC_SEQ + PAST_LEN, LANES), BF16),
                        pltpu.VMEM((DIFF_HEADS, DEC_SEQ + PAST_LEN, 4 * DIFF_DH), BF16)],
        compiler_params=_params("parallel", "arbitrary"),
        name="lat_diff",
    )(lam_p, z, z, z, cache_k, cache_v, *tables, diff_norm.reshape(1, 2 * DIFF_DH))


RET_BLOCK = 256


def _block_diag2(a, b):
    zero = jnp.zeros_like(a)
    return jnp.concatenate([jnp.concatenate([a, zero], axis=1), jnp.concatenate([zero, b], axis=1)], axis=0)


def _ret_kernel(*refs, n_chunks, chained):
    th_ref, q_ref, k_ref, v_ref, g_ref, gn_ref = refs[:6]
    if chained:
        s0f_ref, s0b_ref, y_ref = refs[6:]
    else:
        y_ref, sf_ref, sb_ref = refs[6:]
    C = RET_BLOCK
    dh = RET_DH
    rel = (lax.broadcasted_iota(jnp.int32, (C, C), 0) - lax.broadcasted_iota(jnp.int32, (C, C), 1)).astype(F32)
    pos = lax.broadcasted_iota(jnp.int32, (C, LANES), 0).astype(F32)
    first = lax.broadcasted_iota(jnp.int32, (C, LANES), 1) < dh
    lg = jnp.log1p(-jnp.exp(th_ref[0]))
    lgf, lgb = lg[0:1, :], lg[1:2, :]
    decays = [jnp.exp(jnp.where(rel >= 0, rel * lgf[:, c0:c0 + 1], -rel * lgb[:, c0:c0 + 1])) for c0 in (0, dh)]
    xi_f = jnp.exp((pos + 1.0) * lgf)
    zeta_f = jnp.exp((C - 1.0 - pos) * lgf)
    xi_b = jnp.exp((C - pos) * lgb)
    zeta_b = jnp.exp(pos * lgb)
    gn = gn_ref[...]
    rows = [pl.ds(c * C, C) for c in range(n_chunks)]
    ks = [k_ref[r, :] * (dh ** -0.5) for r in rows]
    vs = [v_ref[r, :] for r in rows]
    uf = [_dot_tn(k * zeta_f, v) for k, v in zip(ks, vs)]
    ub = [_dot_tn(k * zeta_b, v) for k, v in zip(ks, vs)]
    if chained:
        diag = ((lax.broadcasted_iota(jnp.int32, (LANES, LANES), 0) < dh)
                == (lax.broadcasted_iota(jnp.int32, (LANES, LANES), 1) < dh))
        gc_f = jnp.exp(C * lgf)
        gc_b = jnp.exp(C * lgb)
        s_f = [_block_diag2(s0f_ref[0, 0], s0f_ref[0, 1])]
        for c in range(n_chunks - 1):
            s_f.append(gc_f * s_f[c] + jnp.where(diag, uf[c], 0.0))
        s_b = [_block_diag2(s0b_ref[0, 0], s0b_ref[0, 1])]
        for c in range(n_chunks - 1, 0, -1):
            s_b.append(gc_b * s_b[-1] + jnp.where(diag, ub[c], 0.0))
        s_b = s_b[::-1]
    for c in range(n_chunks):
        q = q_ref[rows[c], :]
        kb = ks[c].astype(BF16)
        vb = vs[c].astype(BF16)
        o_heads = [jnp.dot((_dot_nt(jnp.where(keep, q, 0.0), kb) * decay).astype(BF16), vb,
                           preferred_element_type=F32) for keep, decay in zip((first, ~first), decays)]
        o = jnp.where(first, o_heads[0], o_heads[1])
        if chained:
            o = o + _dot(q, s_f[c]) * xi_f + _dot(q, s_b[c]) * xi_b
        else:
            for hh in range(2):
                sf_ref[c, hh] = uf[c][hh * dh:(hh + 1) * dh, hh * dh:(hh + 1) * dh]
                sb_ref[c, hh] = ub[c][hh * dh:(hh + 1) * dh, hh * dh:(hh + 1) * dh]
        sq = o * o
        ms = jnp.where(first, jnp.sum(jnp.where(first, sq, 0.0), axis=-1, keepdims=True),
                       jnp.sum(jnp.where(first, 0.0, sq), axis=-1, keepdims=True)) * (1.0 / dh)
        g = g_ref[rows[c], :]
        y_ref[rows[c], :] = o * lax.rsqrt(ms + EPS) * gn * (g * _sigmoid(g))


def _retention(z, theta, ret_norm, n_batch, t_len, state=None):
    chained = state is not None
    n_chunks = t_len // RET_BLOCK if chained else 4
    rows = n_chunks * RET_BLOCK
    n_steps = n_batch * t_len // rows
    blk = lambda c0: pl.BlockSpec((rows, LANES), lambda b, p: (b, c0 // LANES + p))
    in_specs = [pl.BlockSpec((1, 2, LANES), lambda b, p: (p, 0, 0)),
                blk(COL_RT_Q), blk(COL_RT_K), blk(COL_RT_V), blk(COL_RT_G),
                pl.BlockSpec((1, LANES), lambda b, p: (0, p))]
    args = [theta, z, z, z, z, ret_norm.reshape(1, RET_HEADS * RET_DH)]
    y_spec = pl.BlockSpec((rows, LANES), lambda b, p: (b, p))
    y_shape = jax.ShapeDtypeStruct((n_batch * t_len, BRANCH_W), F32)
    if chained:
        sblk = pl.BlockSpec((1, 2, RET_DH, RET_DH), lambda b, p: (b, p, 0, 0))
        in_specs += [sblk, sblk]
        args += list(state)
        out_specs, out_shape = y_spec, y_shape
    else:
        assert t_len == RET_BLOCK
        sblk = pl.BlockSpec((n_chunks, 2, RET_DH, RET_DH), lambda b, p: (b, p, 0, 0))
        s_shape = jax.ShapeDtypeStruct((n_batch, RET_HEADS, RET_DH, RET_DH), F32)
        out_specs, out_shape = [y_spec, sblk, sblk], [y_shape, s_shape, s_shape]
    return pl.pallas_call(
        functools.partial(_ret_kernel, n_chunks=n_chunks, chained=chained),
        grid=(n_steps, RET_HEADS // 2),
        in_specs=in_specs, out_specs=out_specs, out_shape=out_shape,
        compiler_params=_params("parallel", "parallel"),
        name="retention_lat" if chained else "retention_ctx",
    )(*args)


def _shift_rows(x, k, fill, up):
    t_len = x.shape[0]
    if k % SUBLANES == 0:
        pad = jnp.full((k, x.shape[1]), fill, x.dtype)
        return jnp.concatenate([x[k:], pad], axis=0) if up else jnp.concatenate([pad, x[:t_len - k]], axis=0)
    row = lax.broadcasted_iota(jnp.int32, x.shape, 0)
    if up:
        return jnp.where(row < t_len - k, pltpu.roll(x, t_len - k, axis=0), fill)
    return jnp.where(row >= k, pltpu.roll(x, k, axis=0), fill)


def _linear_scan(a, u, up):
    t_len = a.shape[0]
    k = 1
    while k < t_len:
        u = a * _shift_rows(u, k, 0.0, up) + u
        if 2 * k < t_len:
            a = a * _shift_rows(a, k, 1.0, up)
        k *= 2
    return u


def _lru_kernel(*refs, has_state, write_state):
    x_ref, g_ref, vec_ref, w_ref = refs[:4]
    refs = refs[4:]
    if has_state:
        h0f_ref, h0b_ref = refs[:2]
        refs = refs[2:]
    y_ref = refs[0]
    if write_state:
        hf_ref, hb_ref = refs[1:3]
    x = x_ref[...]
    t_len = x.shape[0]
    row = lax.broadcasted_iota(jnp.int32, x.shape, 0)
    vec = lambda i: vec_ref[i:i + 1, :]
    xd = (vec(0) * _shift_rows(x, 1, 0.0, False) + vec(1) * x + vec(2) * _shift_rows(x, 1, 0.0, True)
          + vec(3) * _shift_rows(x, 2, 0.0, True) + vec(4))

    def gates(wa, wx, ba, bx, lam):
        r = _sigmoid(_dot(xd, wa) + ba)
        i = _sigmoid(_dot(xd, wx) + bx)
        nl = -lam
        softplus = jnp.maximum(nl, 0.0) + jnp.log1p(jnp.exp(-jnp.abs(nl)))
        log_a = -LRU_C * r * softplus
        a = jnp.exp(log_a)
        return a, jnp.sqrt(-jnp.tanh(log_a) * (a * a + 1.0)) * (i * xd)

    a, u = gates(w_ref[0, 0], w_ref[1, 0], vec(5), vec(6), vec(7))
    if has_state:
        u = u + jnp.where(row == 0, a * h0f_ref[0], 0.0)
    h_f = _linear_scan(a, u, False)
    a, u = gates(w_ref[2, 0], w_ref[3, 0], vec(8), vec(9), vec(10))
    if has_state:
        u = u + jnp.where(row == t_len - 1, a * h0b_ref[0], 0.0)
    h_b = _linear_scan(a, u, True)
    g = g_ref[...]
    gelu = 0.5 * g * (1.0 + jnp.tanh(math.sqrt(2.0 / math.pi) * (g + 0.044715 * (g * g * g))))
    y_ref[...] = (h_f + h_b) * gelu
    if write_state:
        hf_ref[0] = h_f[t_len - 1:t_len, :]
        hb_ref[0] = h_b[0:1, :]


def _rglru(z, vecs, w_gate, n_batch, t_len, state=None):
    blk = lambda c0: pl.BlockSpec((t_len, LANES), lambda b, j: (b, c0 // LANES + j))
    hblk = pl.BlockSpec((1, 1, LANES), lambda b, j: (b, 0, j))
    in_specs = [blk(COL_LR_X), blk(COL_LR_G),
                pl.BlockSpec((11, LANES), lambda b, j: (0, j)),
                pl.BlockSpec((4, 1, LANES, LANES), lambda b, j: (0, j, 0, 0))]
    args = [z, z, vecs, w_gate]
    y_spec = pl.BlockSpec((t_len, LANES), lambda b, j: (b, j))
    y_shape = jax.ShapeDtypeStruct((n_batch * t_len, BRANCH_W), F32)
    if state is not None:
        in_specs += [hblk, hblk]
        args += [s.reshape(n_batch, 1, LRU_WIDTH) for s in state]
        out_specs, out_shape = y_spec, y_shape
    else:
        h_shape = jax.ShapeDtypeStruct((n_batch, 1, LRU_WIDTH), F32)
        out_specs, out_shape = [y_spec, hblk, hblk], [y_shape, h_shape, h_shape]
    return pl.pallas_call(
        functools.partial(_lru_kernel, has_state=state is not None, write_state=state is None),
        grid=(n_batch, LRU_WIDTH // LANES),
        in_specs=in_specs, out_specs=out_specs, out_shape=out_shape,
        compiler_params=_params("parallel", "parallel"),
        name="rglru_lat" if state is not None else "rglru_ctx",
    )(*args)


def _pair_block_diag(w):
    bw = LRU_WIDTH // LRU_BLOCKS
    w = w.reshape(LRU_BLOCKS // 2, 2, bw, bw)
    zero = jnp.zeros_like(w[:, 0])
    top = jnp.concatenate([w[:, 0], zero], axis=2)
    bot = jnp.concatenate([zero, w[:, 1]], axis=2)
    return jnp.concatenate([top, bot], axis=1)


def _merge_kernel(x_ref, ya_ref, yb_ref, yc_ref, yd_ref, wg01_ref, wg23_ref, wb_ref, wo_ref, gpre_ref, gpost_ref,
                  m_ref, o_ref):
    x = x_ref[...]
    h = (_rms(x, gpre_ref[...]) * (1.0 + m_ref[0, 1:2, :]) + m_ref[0, 0:1, :]).astype(BF16)
    acc = None
    for k, y_ref in enumerate((ya_ref, yb_ref, yc_ref, yd_ref)):
        wg_ref = wg01_ref if k < 2 else wg23_ref
        logits = jnp.dot(h, wg_ref[0, :, (k % 2) * D_MODEL:(k % 2 + 1) * D_MODEL], preferred_element_type=F32)
        term = _sigmoid(logits) * jnp.dot(y_ref[...].astype(BF16), wb_ref[0, k], preferred_element_type=F32)
        acc = term if acc is None else acc + term
    out = jnp.dot(acc.astype(BF16), wo_ref[0], preferred_element_type=F32)
    o_ref[...] = x + m_ref[0, 2:3, :] * _rms(out, gpost_ref[...])


def _merge(x, ys, mod, w_in_bf, wb_bf, wo_bf, gpre, gpost, l, mod_row):
    n_tok = x.shape[0]
    tm = 256
    yblk = pl.BlockSpec((tm, BRANCH_W), lambda i: (i, 0))
    return pl.pallas_call(
        _merge_kernel,
        grid=(n_tok // tm,),
        in_specs=[pl.BlockSpec((tm, D_MODEL), lambda i: (i, 0)), yblk, yblk, yblk, yblk,
                  pl.BlockSpec((1, D_MODEL, 2 * D_MODEL), lambda i: (l, 0, COL_GATE // (2 * D_MODEL))),
                  pl.BlockSpec((1, D_MODEL, 2 * D_MODEL), lambda i: (l, 0, COL_GATE // (2 * D_MODEL) + 1)),
                  pl.BlockSpec((1, N_BRANCH, BRANCH_W, D_MODEL), lambda i: (l, 0, 0, 0)),
                  pl.BlockSpec((1, D_MODEL, D_MODEL), lambda i: (l, 0, 0)),
                  pl.BlockSpec((1, D_MODEL), lambda i: (0, 0)),
                  pl.BlockSpec((1, D_MODEL), lambda i: (0, 0)),
                  pl.BlockSpec((1, 6, D_MODEL), lambda i: (mod_row(i, tm), 0, 0))],
        out_specs=pl.BlockSpec((tm, D_MODEL), lambda i: (i, 0)),
        out_shape=jax.ShapeDtypeStruct((n_tok, D_MODEL), F32),
        compiler_params=_params("parallel"),
        name="merge",
    )(x, *ys, w_in_bf, w_in_bf, wb_bf, wo_bf, gpre.reshape(1, D_MODEL), gpost.reshape(1, D_MODEL), mod)


def _mlp_kernel(x_ref, m_ref, gpre_ref, gpost_ref, w1_ref, w2_ref, o_ref):
    x = x_ref[...]
    h = (_rms(x, gpre_ref[...]) * (1.0 + m_ref[0, 4:5, :]) + m_ref[0, 3:4, :]).astype(BF16)
    ff_chunk = D_MODEL
    y = None
    for j in range(D_FF // ff_chunk):
        sl = slice(j * ff_chunk, (j + 1) * ff_chunk)
        a = jnp.maximum(jnp.dot(h, w1_ref[0, :, sl], preferred_element_type=F32), 0.0)
        part = jnp.dot((a * a).astype(BF16), w2_ref[0, sl, :], preferred_element_type=F32)
        y = part if y is None else y + part
    o_ref[...] = x + m_ref[0, 5:6, :] * _rms(y, gpost_ref[...])


def _mlp(x, mod, gpre, gpost, w1_bf, w2_bf, l, mod_row):
    n_tok = x.shape[0]
    tm = 256
    return pl.pallas_call(
        _mlp_kernel,
        grid=(n_tok // tm,),
        in_specs=[pl.BlockSpec((tm, D_MODEL), lambda i: (i, 0)),
                  pl.BlockSpec((1, 6, D_MODEL), lambda i: (mod_row(i, tm), 0, 0)),
                  pl.BlockSpec((1, D_MODEL), lambda i: (0, 0)),
                  pl.BlockSpec((1, D_MODEL), lambda i: (0, 0)),
                  pl.BlockSpec((1, D_MODEL, D_FF), lambda i: (l, 0, 0)),
                  pl.BlockSpec((1, D_FF, D_MODEL), lambda i: (l, 0, 0))],
        out_specs=pl.BlockSpec((tm, D_MODEL), lambda i: (i, 0)),
        out_shape=jax.ShapeDtypeStruct((n_tok, D_MODEL), F32),
        compiler_params=_params("parallel"),
        name="mlp",
    )(x, mod, gpre.reshape(1, D_MODEL), gpost.reshape(1, D_MODEL), w1_bf, w2_bf)


def _cache_heads_kernel(*refs, n_heads, dh, transposed):
    o_ref = refs[-1]
    for l, z_ref in enumerate(refs[:-1]):
        if transposed:
            for p in range(n_heads * dh // LANES):
                zt = z_ref[:, p * LANES:(p + 1) * LANES].T
                for i in range(LANES // dh):
                    o_ref[0, l, p * (LANES // dh) + i] = zt[i * dh:(i + 1) * dh, :]
        else:
            for h in range(n_heads):
                o_ref[0, l, h] = z_ref[:, h * dh:(h + 1) * dh]


def _cache_heads(zs, col, n_heads, dh):
    transposed = dh < LANES
    tail = (dh, SEQ) if transposed else (SEQ, dh)
    out = pl.pallas_call(
        functools.partial(_cache_heads_kernel, n_heads=n_heads, dh=dh, transposed=transposed),
        grid=(BATCH,),
        in_specs=[pl.BlockSpec((SEQ, 512), lambda b: (b, col // 512))] * DEPTH,
        out_specs=pl.BlockSpec((1, DEPTH, n_heads) + tail, lambda b: (b, 0, 0, 0, 0)),
        out_shape=jax.ShapeDtypeStruct((BATCH, DEPTH, n_heads) + tail, F32),
        compiler_params=_params("parallel"),
        name="cache_heads",
    )(*zs)
    return jnp.swapaxes(out, -1, -2) if transposed else out


def _ctx_mod_row(i, tm):
    return 0


def _lat_mod_row(i, tm):
    return 1 + i // (DEC_SEQ // tm)


def kernel(x_prompt, x_sample, cache_na_k, cache_na_v, cache_diff_k, cache_diff_v, state_ret_fwd, state_ret_bwd, state_lru_fwd, state_lru_bwd, c, c_ctx, ada_w, ada_b, norm_mix_pre, norm_mix_post, norm_ffn_pre, norm_ffn_post, w_in, na_rpb, diff_lq1, diff_lk1, diff_lq2, diff_lk2, diff_norm, ret_theta_fwd, ret_theta_bwd, ret_norm, lru_conv_w, lru_conv_b, lru_wa_fwd, lru_ba_fwd, lru_wx_fwd, lru_bx_fwd, lru_lam_fwd, lru_wa_bwd, lru_ba_bwd, lru_wx_bwd, lru_bx_bwd, lru_lam_bwd, w_branch, w_out, mlp_w1, mlp_w2):
    xc = x_prompt.reshape(BATCH * SEQ, D_MODEL)
    xl = x_sample.reshape(DEC_BATCH * DEC_SEQ, D_MODEL)
    cond = jnp.concatenate([c_ctx[None, :], c, jnp.zeros((COND_PAD - N_COND, D_MODEL), F32)], axis=0)
    mods = _modulation(cond.T, ada_w, ada_b).reshape(DEPTH, COND_PAD, 6, D_MODEL)
    rope_tables = _rope_tables()
    pair_tables = _na_pair_tables(_na_col_tables(na_rpb))
    w_in_bf = w_in.astype(BF16)
    wb_bf = w_branch.astype(BF16)
    wo_bf = w_out.astype(BF16)
    w1_bf = mlp_w1.astype(BF16)
    w2_bf = mlp_w2.astype(BF16)
    na_kt = jnp.swapaxes(cache_na_k, -1, -2)
    na_vt = jnp.swapaxes(cache_na_v, -1, -2)
    diff_kt = jnp.swapaxes(cache_diff_k, -1, -2)
    outs = [[] for _ in range(4)]
    zcs = []
    for l in range(DEPTH):
        mod = mods[l]
        lam_init = 0.8 - 0.6 * math.exp(-0.3 * l)
        lam_p = jnp.stack([diff_lq1[l], diff_lk1[l], diff_lq2[l], diff_lk2[l]])
        theta = jnp.stack([jnp.repeat(t, RET_DH).reshape(RET_HEADS // 2, LANES)
                           for t in (ret_theta_fwd[l], ret_theta_bwd[l])], axis=1)
        vecs = jnp.concatenate([lru_conv_w[l], lru_conv_b[l][None],
                                lru_ba_fwd[l][None], lru_bx_fwd[l][None], lru_lam_fwd[l][None],
                                lru_ba_bwd[l][None], lru_bx_bwd[l][None], lru_lam_bwd[l][None]], axis=0)
        w_gate = jnp.stack([_pair_block_diag(w) for w in
                            (lru_wa_fwd[l], lru_wx_fwd[l], lru_wa_bwd[l], lru_wx_bwd[l])])

        zc = _inproj(xc, mod, norm_mix_pre[l], w_in_bf, l, _ctx_mod_row)
        ya = _ctx_na(zc)
        yb = _ctx_diff(zc, lam_p, diff_norm[l], lam_init)
        yc, s_rf, s_rb = _retention(zc, theta, ret_norm[l], BATCH, SEQ)
        yd, h_lf, h_lb = _rglru(zc, vecs, w_gate, BATCH, SEQ)
        xc = _merge(xc, (ya, yb, yc, yd), mod, w_in_bf, wb_bf, wo_bf, norm_mix_pre[l], norm_mix_post[l], l,
                    _ctx_mod_row)
        xc = _mlp(xc, mod, norm_ffn_pre[l], norm_ffn_post[l], w1_bf, w2_bf, l, _ctx_mod_row)
        zcs.append(zc)
        outs[0].append(s_rf)
        outs[1].append(s_rb)
        outs[2].append(h_lf.reshape(BATCH, LRU_WIDTH))
        outs[3].append(h_lb.reshape(BATCH, LRU_WIDTH))

        zl = _inproj(xl, mod, norm_mix_pre[l], w_in_bf, l, _lat_mod_row)
        ya = _lat_na(zl, na_kt, na_vt, pair_tables, l)
        yb = _lat_diff(zl, diff_kt, cache_diff_v, lam_p, diff_norm[l], rope_tables, lam_init, l)
        yc = _retention(zl, theta, ret_norm[l], DEC_BATCH, DEC_SEQ,
                        state=(state_ret_fwd[:, l], state_ret_bwd[:, l]))
        yd = _rglru(zl, vecs, w_gate, DEC_BATCH, DEC_SEQ, state=(state_lru_fwd[:, l], state_lru_bwd[:, l]))
        xl = _merge(xl, (ya, yb, yc, yd), mod, w_in_bf, wb_bf, wo_bf, norm_mix_pre[l], norm_mix_post[l], l,
                    _lat_mod_row)
        xl = _mlp(xl, mod, norm_ffn_pre[l], norm_ffn_post[l], w1_bf, w2_bf, l, _lat_mod_row)

    new_na_k = _cache_heads(zcs, COL_NA_K, NA_HEADS, NA_DH)
    new_na_v = _cache_heads(zcs, COL_NA_V, NA_HEADS, NA_DH)
    new_diff_k = _cache_heads(zcs, COL_DF_K, 2 * DIFF_HEADS, DIFF_DH).reshape(
        BATCH, DEPTH, 2, DIFF_HEADS, SEQ, DIFF_DH)
    new_diff_v = _cache_heads(zcs, COL_DF_V, DIFF_HEADS, 2 * DIFF_DH)
    states = [jnp.stack(o, axis=1) for o in outs]
    return (xc.reshape(BATCH, SEQ, D_MODEL), xl.reshape(DEC_BATCH, DEC_SEQ, D_MODEL),
            new_na_k, new_na_v, new_diff_k, new_diff_v, *states)
```

```python
import functools
import math

import numpy as np
import jax
import jax.numpy as jnp
from jax import lax
from jax.experimental import pallas as pl
from jax.experimental.pallas import tpu as pltpu

F32 = jnp.float32
BF16 = jnp.bfloat16

D_MODEL = 1024
BATCH = 32
SEQ = 256
DEPTH = 4
DEC_BATCH = 4
DEC_SEQ = 2048
PAST_LEN = 512
GRID_W = 64
NA_HEADS = 8
NA_DH = 64
NA_ROWS = 8
NA_COLS = 16
DIFF_HEADS = 4
DIFF_DH = 64
RET_HEADS = 8
RET_DH = 64
LRU_WIDTH = 512
LRU_BLOCKS = 8
LRU_CONV = 4
LRU_C = 8.0
N_BRANCH = 4
BRANCH_W = 512
D_FF = 4 * D_MODEL
ROPE_BASE = 10000.0
EPS = 1e-6
NEG_INF = -1e30

COL_NA_Q, COL_NA_K, COL_NA_V = 0, 512, 1024
COL_DF_Q, COL_DF_K, COL_DF_V = 1536, 2048, 2560
COL_RT_Q, COL_RT_K, COL_RT_V, COL_RT_G = 3072, 3584, 4096, 4608
COL_LR_X, COL_LR_G = 5120, 5632
COL_GATE = 6144
Z_WIDTH = COL_GATE

N_COND = 1 + DEC_BATCH
COND_PAD = 8
LANES = 128
SUBLANES = 8
VMEM_LIMIT = 56 * 1024 * 1024


def _params(*sem):
    return pltpu.CompilerParams(dimension_semantics=sem, vmem_limit_bytes=VMEM_LIMIT)


def _dot(a, b):
    return jnp.dot(a.astype(BF16), b.astype(BF16), preferred_element_type=F32)


def _dot_nt(a, b):
    return lax.dot_general(a.astype(BF16), b.astype(BF16), (((1,), (1,)), ((), ())),
                           preferred_element_type=F32)


def _dot_tn(a, b):
    return lax.dot_general(a.astype(BF16), b.astype(BF16), (((0,), (0,)), ((), ())),
                           preferred_element_type=F32)


def _rms(x, g):
    return x * lax.rsqrt(jnp.mean(x * x, axis=-1, keepdims=True) + EPS) * g


def _sigmoid(x):
    return 1.0 / (1.0 + jnp.exp(-x))


LOG2E = math.log2(math.e)


def _exp2_weights(parts):
    m = functools.reduce(jnp.maximum, [jnp.max(s, axis=-1, keepdims=True) for s in parts])
    return [jnp.exp2(s - m).astype(BF16) for s in parts]


def _with_ones(v):
    return jnp.concatenate([v.astype(BF16), jnp.ones(v.shape, BF16)], axis=1)


def _normalised(r):
    w = r.shape[1] // 2
    return r[:, :w] / r[:, w:]


def _mod_kernel(ct_ref, w_ref, b_ref, o_ref):
    ct = ct_ref[...]
    s = ct * _sigmoid(ct)
    w = w_ref[0]
    rows = [jnp.sum(s[:, r:r + 1] * w, axis=0, keepdims=True) + b_ref[0] for r in range(N_COND)]
    rows.append(jnp.zeros((COND_PAD - N_COND, w.shape[1]), F32))
    o_ref[0] = jnp.concatenate(rows, axis=0)


def _modulation(cond_t, ada_w, ada_b):
    tn = 768
    n = 6 * D_MODEL
    return pl.pallas_call(
        _mod_kernel,
        grid=(DEPTH, n // tn),
        in_specs=[pl.BlockSpec((D_MODEL, COND_PAD), lambda l, j: (0, 0)),
                  pl.BlockSpec((1, D_MODEL, tn), lambda l, j: (l, 0, j)),
                  pl.BlockSpec((1, 1, tn), lambda l, j: (l, 0, j))],
        out_specs=pl.BlockSpec((1, COND_PAD, tn), lambda l, j: (l, 0, j)),
        out_shape=jax.ShapeDtypeStruct((DEPTH, COND_PAD, n), F32),
        compiler_params=_params("parallel", "parallel"),
        name="modulation",
    )(cond_t, ada_w, ada_b.reshape(DEPTH, 1, n))


def _inproj_kernel(x_ref, m_ref, g_ref, w_ref, z_ref, h_scr):
    @pl.when(pl.program_id(1) == 0)
    def _():
        h = _rms(x_ref[...], g_ref[...]) * (1.0 + m_ref[0, 1:2, :]) + m_ref[0, 0:1, :]
        h_scr[...] = h.astype(BF16)

    z_ref[...] = jnp.dot(h_scr[...], w_ref[0], preferred_element_type=F32)


def _inproj(x, mod, gain, w_bf, l, mod_row):
    n_tok = x.shape[0]
    tm, tn = 2048, 1024
    return pl.pallas_call(
        _inproj_kernel,
        grid=(n_tok // tm, Z_WIDTH // tn),
        in_specs=[pl.BlockSpec((tm, D_MODEL), lambda i, j: (i, 0)),
                  pl.BlockSpec((1, 6, D_MODEL), lambda i, j: (mod_row(i, tm), 0, 0)),
                  pl.BlockSpec((1, D_MODEL), lambda i, j: (0, 0)),
                  pl.BlockSpec((1, D_MODEL, tn), lambda i, j: (l, 0, j))],
        out_specs=pl.BlockSpec((tm, tn), lambda i, j: (i, j)),
        out_shape=jax.ShapeDtypeStruct((n_tok, Z_WIDTH), F32),
        scratch_shapes=[pltpu.VMEM((tm, D_MODEL), BF16)],
        compiler_params=_params("parallel", "arbitrary"),
        name="inproj",
    )(x, mod, gain.reshape(1, D_MODEL), w_bf)


def _ctx_na_kernel(q_ref, k_ref, v_ref, o_ref):
    qs = NA_DH ** -0.5 * LOG2E
    first = lax.broadcasted_iota(jnp.int32, (SEQ, LANES), 1) < NA_DH
    for p in range(NA_HEADS * NA_DH // LANES):
        sl = slice(p * LANES, (p + 1) * LANES)
        q = q_ref[:, sl] * qs
        k = k_ref[:, sl].astype(BF16)
        vx = _with_ones(v_ref[:, sl])
        att = []
        for keep in (first, ~first):
            (e,) = _exp2_weights([_dot_nt(jnp.where(keep, q, 0.0), k)])
            att.append(_normalised(jnp.dot(e, vx, preferred_element_type=F32)))
        o_ref[:, sl] = jnp.where(first, att[0], att[1])


def _ctx_na(z):
    blk = lambda j: pl.BlockSpec((SEQ, 512), lambda b: (b, j))
    return pl.pallas_call(
        _ctx_na_kernel,
        grid=(BATCH,),
        in_specs=[blk(COL_NA_Q // 512), blk(COL_NA_K // 512), blk(COL_NA_V // 512)],
        out_specs=pl.BlockSpec((SEQ, 512), lambda b: (b, 0)),
        out_shape=jax.ShapeDtypeStruct((BATCH * SEQ, BRANCH_W), F32),
        compiler_params=_params("parallel"),
        name="ctx_na",
    )(z, z, z)


def _diff_lambda(lam_ref, lam_init):
    a = jnp.sum(lam_ref[0:1, :] * lam_ref[1:2, :], axis=-1, keepdims=True)
    b = jnp.sum(lam_ref[2:3, :] * lam_ref[3:4, :], axis=-1, keepdims=True)
    return jnp.exp(a) - jnp.exp(b) + lam_init


def _ctx_diff_kernel(lam_ref, q_ref, k_ref, v_ref, dn_ref, o_ref, *, lam_init):
    qs = DIFF_DH ** -0.5 * LOG2E
    lam = _diff_lambda(lam_ref, lam_init)
    w = 2 * DIFF_DH
    first = lax.broadcasted_iota(jnp.int32, (SEQ, LANES), 1) < DIFF_DH
    for h in range(DIFF_HEADS):
        vx = _with_ones(v_ref[:, h * w:(h + 1) * w])
        keep = first if h % 2 == 0 else ~first
        att = []
        for c in range(2):
            c0 = (c * DIFF_HEADS + h - h % 2) * DIFF_DH
            sl = slice(c0, c0 + LANES)
            (e,) = _exp2_weights([_dot_nt(jnp.where(keep, q_ref[:, sl] * qs, 0.0), k_ref[:, sl])])
            att.append(_normalised(jnp.dot(e, vx, preferred_element_type=F32)))
        o = att[0] - lam * att[1]
        o_ref[:, h * w:(h + 1) * w] = _rms(o, dn_ref[...]) * (1.0 - lam_init)


def _ctx_diff(z, lam_p, diff_norm, lam_init):
    blk = lambda j: pl.BlockSpec((SEQ, 512), lambda b: (b, j))
    return pl.pallas_call(
        functools.partial(_ctx_diff_kernel, lam_init=lam_init),
        grid=(BATCH,),
        in_specs=[pl.BlockSpec((4, DIFF_DH), lambda b: (0, 0)),
                  blk(COL_DF_Q // 512), blk(COL_DF_K // 512), blk(COL_DF_V // 512),
                  pl.BlockSpec((1, 2 * DIFF_DH), lambda b: (0, 0))],
        out_specs=pl.BlockSpec((SEQ, 512), lambda b: (b, 0)),
        out_shape=jax.ShapeDtypeStruct((BATCH * SEQ, BRANCH_W), F32),
        compiler_params=_params("parallel"),
        name="ctx_diff",
    )(lam_p, z, z, z, diff_norm.reshape(1, 2 * DIFF_DH))


NA_GRID_ROWS = DEC_SEQ // GRID_W
NA_KEY_ROWS = min(NA_ROWS, NA_GRID_ROWS)


NA_DR = 2 * NA_ROWS - 1
NA_DC = 2 * NA_COLS - 1


def _na_col_kernel(rpb_ref, idx_ref, o_ref):
    idx = jnp.broadcast_to(idx_ref[...], o_ref.shape)
    acc = jnp.full(o_ref.shape, NEG_INF, F32)
    for d in range(NA_DC):
        acc = jnp.where(idx == d, jnp.broadcast_to(rpb_ref[:, d:d + 1], o_ref.shape), acc)
    o_ref[...] = acc * LOG2E


def _na_col_tables(na_rpb):
    qc = np.arange(GRID_W)[:, None]
    kc = np.arange(GRID_W)[None, :]
    cstart = np.clip(qc - NA_COLS // 2, 0, GRID_W - NA_COLS)
    ok = (kc >= cstart) & (kc < cstart + NA_COLS)
    idx = np.where(ok, kc - qc + NA_COLS - 1, -1).astype(np.int32).reshape(1, GRID_W * GRID_W)
    rows = NA_HEADS * NA_DR
    rpb = jnp.pad(na_rpb.reshape(DEPTH * rows, NA_DC), ((0, 0), (0, 32 - NA_DC)))
    t = pl.pallas_call(
        _na_col_kernel,
        grid=(DEPTH,),
        in_specs=[pl.BlockSpec((rows, 32), lambda l: (l, 0)),
                  pl.BlockSpec((1, GRID_W * GRID_W), lambda l: (0, 0))],
        out_specs=pl.BlockSpec((rows, GRID_W * GRID_W), lambda l: (l, 0)),
        out_shape=jax.ShapeDtypeStruct((DEPTH * rows, GRID_W * GRID_W), F32),
        compiler_params=_params("parallel"),
        name="na_col_tables",
    )(rpb, jnp.asarray(idx))
    return t.reshape(DEPTH, NA_HEADS, NA_DR, GRID_W, GRID_W)


NA_QROWS = 4
NA_UROWS = 12
NA_QBLK = NA_QROWS * GRID_W
NA_UKEYS = NA_UROWS * GRID_W
NA_NBLK = NA_GRID_ROWS // NA_QROWS


def _na_union_start(r0):
    return int(np.clip(r0 - NA_KEY_ROWS // 2, 0, NA_GRID_ROWS - NA_UROWS))


def _na_pair_tables(t):
    tp = jnp.pad(t, ((0, 0), (0, 0), (1, 1), (0, 0), (0, 0)), constant_values=NEG_INF)
    return jnp.concatenate([tp[:, :, :-1], tp[:, :, 1:]], axis=-1)


def _na_fill_bias(pair_ref, bias_scr):
    first = lax.broadcasted_iota(jnp.int32, (GRID_W, LANES), 1) < GRID_W
    neg = jnp.full((GRID_W, LANES), NEG_INF, F32)
    for variant, r0 in enumerate((0, NA_QROWS, NA_GRID_ROWS - NA_QROWS)):
        us = _na_union_start(r0)
        for j in range(NA_QROWS):
            r = r0 + j
            rs = int(np.clip(r - NA_KEY_ROWS // 2, 0, NA_GRID_ROWS - NA_KEY_ROWS))
            for m in range(NA_UROWS // 2):
                ok = [rs <= us + i < rs + NA_KEY_ROWS for i in (2 * m, 2 * m + 1)]
                p = us + 2 * m - r + NA_ROWS
                for hh in range(2):
                    if ok[0] and ok[1]:
                        tile = pair_ref[0, hh, p]
                    elif ok[0]:
                        tile = jnp.where(first, pair_ref[0, hh, p], NEG_INF)
                    elif ok[1]:
                        tile = jnp.where(first, NEG_INF, pair_ref[0, hh, p])
                    else:
                        tile = neg
                    bias_scr[variant, hh, j * GRID_W:(j + 1) * GRID_W, m * LANES:(m + 1) * LANES] = tile


def _lat_na_kernel(q_ref, k_ref, v_ref, kc_ref, vc_ref, pair_ref, o_ref, k_scr, vx_scr, kc_scr, vcx_scr, bias_ref):
    qs = NA_DH ** -0.5 * LOG2E
    _na_fill_bias(pair_ref, bias_ref)
    k_scr[...] = k_ref[...].astype(BF16)
    vx_scr[...] = _with_ones(v_ref[...])
    kc_scr[...] = jnp.concatenate([kc_ref[0, 0, 0], kc_ref[0, 0, 1]], axis=0).astype(BF16)
    vcx_scr[...] = jnp.concatenate([vc_ref[0, 0, 0].astype(BF16), vc_ref[0, 0, 1].astype(BF16),
                                    jnp.ones((LANES, PAST_LEN), BF16)], axis=0)
    first = lax.broadcasted_iota(jnp.int32, (NA_QBLK, LANES), 1) < NA_DH

    for blk in range(NA_NBLK):
        us = _na_union_start(blk * NA_QROWS)
        variant = 0 if blk == 0 else (2 if blk == NA_NBLK - 1 else 1)
        rows = slice(blk * NA_QBLK, (blk + 1) * NA_QBLK)
        keys = slice(us * GRID_W, us * GRID_W + NA_UKEYS)
        q = q_ref[rows, :] * qs
        att = []
        for hh, keep in enumerate((first, ~first)):
            qm = jnp.where(keep, q, 0.0).astype(BF16)
            s_loc = _dot_nt(qm, k_scr[keys, :]) + bias_ref[variant, hh]
            s_ctx = jnp.dot(qm, kc_scr[...], preferred_element_type=F32)
            e_loc, e_ctx = _exp2_weights([s_loc, s_ctx])
            att.append(_normalised(jnp.dot(e_loc, vx_scr[keys, :], preferred_element_type=F32)
                                   + _dot_nt(e_ctx, vcx_scr[...])))
        o_ref[rows, :] = jnp.where(first, att[0], att[1])


def _lat_na(z, cache_k, cache_v, pair_tables, l):
    blk = lambda c0: pl.BlockSpec((DEC_SEQ, LANES), lambda p, b: (b, c0 // LANES + p))
    cblk = pl.BlockSpec((1, 1, 2, NA_DH, PAST_LEN), lambda p, b: (b, l, p, 0, 0))
    return pl.pallas_call(
        _lat_na_kernel,
        grid=(NA_HEADS // 2, DEC_BATCH),
        in_specs=[blk(COL_NA_Q), blk(COL_NA_K), blk(COL_NA_V), cblk, cblk,
                  pl.BlockSpec((1, 2, NA_DR + 1, GRID_W, LANES), lambda p, b: (l, p, 0, 0, 0))],
        out_specs=pl.BlockSpec((DEC_SEQ, LANES), lambda p, b: (b, p)),
        out_shape=jax.ShapeDtypeStruct((DEC_BATCH * DEC_SEQ, BRANCH_W), F32),
        scratch_shapes=[pltpu.VMEM((DEC_SEQ, LANES), BF16), pltpu.VMEM((DEC_SEQ, 2 * LANES), BF16),
                        pltpu.VMEM((LANES, PAST_LEN), BF16), pltpu.VMEM((2 * LANES, PAST_LEN), BF16),
                        pltpu.VMEM((3, 2, NA_QBLK, NA_UKEYS), F32)],
        compiler_params=_params("parallel", "parallel"),
        name="lat_na",
    )(z, z, z, cache_k, cache_v, pair_tables)


def _rope_tables():
    t = jnp.arange(DEC_SEQ)
    quarter = DIFF_DH // 4
    inv = ROPE_BASE ** (-jnp.arange(quarter, dtype=F32) / quarter)
    ang_r = (t // GRID_W).astype(F32)[:, None] * inv
    ang_c = (t % GRID_W).astype(F32)[:, None] * inv
    lane = np.arange(LANES) % DIFF_DH
    idx = lane % quarter
    use_r = jnp.asarray(lane < DIFF_DH // 2)[None, :]
    first = jnp.asarray((lane % (2 * quarter)) < quarter)[None, :]
    ang = jnp.where(use_r, ang_r[:, idx], ang_c[:, idx])
    cos = jnp.cos(ang)
    sin = jnp.sin(ang)
    return cos, jnp.where(first, -sin, 0.0), jnp.where(first, 0.0, sin)


def _rope(x, cos, sin_a, sin_b):
    quarter = DIFF_DH // 4
    up = pltpu.roll(x, LANES - quarter, axis=1)
    down = pltpu.roll(x, quarter, axis=1)
    return x * cos + up * sin_a + down * sin_b


def _lat_diff_kernel(lam_ref, q_ref, k_ref, v_ref, kc_ref, vc_ref, cos_ref, sa_ref, sb_ref, dn_ref, o_ref,
                     k_scr, vx_scr, *, lam_init, tq):
    qs = DIFF_DH ** -0.5 * LOG2E
    n_slab = 512 // LANES
    w = 2 * DIFF_DH
    lat = slice(0, DEC_SEQ)
    ctx = slice(DEC_SEQ, DEC_SEQ + PAST_LEN)

    @pl.when(pl.program_id(1) == 0)
    def _():
        for j in range(n_slab):
            k_scr[j, lat, :] = _rope(k_ref[:, j * LANES:(j + 1) * LANES],
                                     cos_ref[...], sa_ref[...], sb_ref[...]).astype(BF16)
        for c in range(2):
            for hp in range(DIFF_HEADS // 2):
                k_scr[c * (DIFF_HEADS // 2) + hp, ctx, :] = jnp.concatenate(
                    [kc_ref[0, 0, c, 2 * hp], kc_ref[0, 0, c, 2 * hp + 1]], axis=0).T.astype(BF16)
        for h in range(DIFF_HEADS):
            vx_scr[h, lat, :] = _with_ones(v_ref[:, h * w:(h + 1) * w])
            vx_scr[h, ctx, :] = _with_ones(vc_ref[0, 0, h])

    r0 = pl.multiple_of(pl.program_id(1) * tq, tq)
    cos, sa, sb = cos_ref[pl.ds(r0, tq), :], sa_ref[pl.ds(r0, tq), :], sb_ref[pl.ds(r0, tq), :]
    qrot = [_rope(q_ref[:, j * LANES:(j + 1) * LANES], cos, sa, sb) * qs for j in range(n_slab)]
    lam = _diff_lambda(lam_ref, lam_init)
    first = lax.broadcasted_iota(jnp.int32, (tq, LANES), 1) < DIFF_DH
    for h in range(DIFF_HEADS):
        keep = first if h % 2 == 0 else ~first
        att = []
        for c in range(2):
            j = c * (DIFF_HEADS // 2) + h // 2
            (e,) = _exp2_weights([_dot_nt(jnp.where(keep, qrot[j], 0.0), k_scr[j])])
            att.append(_normalised(jnp.dot(e, vx_scr[h], preferred_element_type=F32)))
        o = att[0] - lam * att[1]
        o_ref[:, h * w:(h + 1) * w] = _rms(o, dn_ref[...]) * (1.0 - lam_init)


def _lat_diff(z, cache_k, cache_v, lam_p, diff_norm, tables, lam_init, l):
    tq = 256
    nq = DEC_SEQ // tq
    tab = pl.BlockSpec((DEC_SEQ, LANES), lambda b, i: (0, 0))
    return pl.pallas_call(
        functools.partial(_lat_diff_kernel, lam_init=lam_init, tq=tq),
        grid=(DEC_BATCH, nq),
        in_specs=[pl.BlockSpec((4, DIFF_DH), lambda b, i: (0, 0)),
                  pl.BlockSpec((tq, 512), lambda b, i: (b * nq + i, COL_DF_Q // 512)),
                  pl.BlockSpec((DEC_SEQ, 512), lambda b, i: (b, COL_DF_K // 512)),
                  pl.BlockSpec((DEC_SEQ, 512), lambda b, i: (b, COL_DF_V // 512)),
                  pl.BlockSpec((1, 1, 2, DIFF_HEADS, DIFF_DH, PAST_LEN), lambda b, i: (b, l, 0, 0, 0, 0)),
                  pl.BlockSpec((1, 1, DIFF_HEADS, PAST_LEN, 2 * DIFF_DH), lambda b, i: (b, l, 0, 0, 0)),
                  tab, tab, tab,
                  pl.BlockSpec((1, 2 * DIFF_DH), lambda b, i: (0, 0))],
        out_specs=pl.BlockSpec((tq, 512), lambda b, i: (b * nq + i, 0)),
        out_shape=jax.ShapeDtypeStruct((DEC_BATCH * DEC_SEQ, BRANCH_W), F32),
        scratch_shapes=[pltpu.VMEM((DIFF_HEADS, DEC_SEQ + PAST_LEN, LANES), BF16),
                        pltpu.VMEM((DIFF_HEADS, DEC_SEQ + PAST_LEN, 4 * DIFF_DH), BF16)],
        compiler_params=_params("parallel", "arbitrary"),
        name="lat_diff",
    )(lam_p, z, z, z, cache_k, cache_v, *tables, diff_norm.reshape(1, 2 * DIFF_DH))


RET_BLOCK = 256


def _block_diag2(a, b):
    zero = jnp.zeros_like(a)
    return jnp.concatenate([jnp.concatenate([a, zero], axis=1), jnp.concatenate([zero, b], axis=1)], axis=0)


def _ret_kernel(*refs, n_chunks, chained):
    th_ref, q_ref, k_ref, v_ref, g_ref, gn_ref = refs[:6]
    if chained:
        s0f_ref, s0b_ref, y_ref = refs[6:]
    else:
        y_ref, sf_ref, sb_ref = refs[6:]
    C = RET_BLOCK
    dh = RET_DH
    rel = (lax.broadcasted_iota(jnp.int32, (C, C), 0) - lax.broadcasted_iota(jnp.int32, (C, C), 1)).astype(F32)
    pos = lax.broadcasted_iota(jnp.int32, (C, LANES), 0).astype(F32)
    first = lax.broadcasted_iota(jnp.int32, (C, LANES), 1) < dh
    lg = jnp.log1p(-jnp.exp(th_ref[0]))
    lgf, lgb = lg[0:1, :], lg[1:2, :]
    decays = [jnp.exp(jnp.where(rel >= 0, rel * lgf[:, c0:c0 + 1], -rel * lgb[:, c0:c0 + 1])) for c0 in (0, dh)]
    xi_f = jnp.exp((pos + 1.0) * lgf)
    zeta_f = jnp.exp((C - 1.0 - pos) * lgf)
    xi_b = jnp.exp((C - pos) * lgb)
    zeta_b = jnp.exp(pos * lgb)
    gn = gn_ref[...]
    rows = [pl.ds(c * C, C) for c in range(n_chunks)]
    ks = [k_ref[r, :] * (dh ** -0.5) for r in rows]
    vs = [v_ref[r, :] for r in rows]
    uf = [_dot_tn(k * zeta_f, v) for k, v in zip(ks, vs)]
    ub = [_dot_tn(k * zeta_b, v) for k, v in zip(ks, vs)]
    if chained:
        diag = ((lax.broadcasted_iota(jnp.int32, (LANES, LANES), 0) < dh)
                == (lax.broadcasted_iota(jnp.int32, (LANES, LANES), 1) < dh))
        gc_f = jnp.exp(C * lgf)
        gc_b = jnp.exp(C * lgb)
        s_f = [_block_diag2(s0f_ref[0, 0], s0f_ref[0, 1])]
        for c in range(n_chunks - 1):
            s_f.append(gc_f * s_f[c] + jnp.where(diag, uf[c], 0.0))
        s_b = [_block_diag2(s0b_ref[0, 0], s0b_ref[0, 1])]
        for c in range(n_chunks - 1, 0, -1):
            s_b.append(gc_b * s_b[-1] + jnp.where(diag, ub[c], 0.0))
        s_b = s_b[::-1]
    for c in range(n_chunks):
        q = q_ref[rows[c], :]
        kb = ks[c].astype(BF16)
        vb = vs[c].astype(BF16)
        o_heads = [jnp.dot((_dot_nt(jnp.where(keep, q, 0.0), kb) * decay).astype(BF16), vb,
                           preferred_element_type=F32) for keep, decay in zip((first, ~first), decays)]
        o = jnp.where(first, o_heads[0], o_heads[1])
        if chained:
            o = o + _dot(q, s_f[c]) * xi_f + _dot(q, s_b[c]) * xi_b
        else:
            for hh in range(2):
                sf_ref[c, hh] = uf[c][hh * dh:(hh + 1) * dh, hh * dh:(hh + 1) * dh]
                sb_ref[c, hh] = ub[c][hh * dh:(hh + 1) * dh, hh * dh:(hh + 1) * dh]
        sq = o * o
        ms = jnp.where(first, jnp.sum(jnp.where(first, sq, 0.0), axis=-1, keepdims=True),
                       jnp.sum(jnp.where(first, 0.0, sq), axis=-1, keepdims=True)) * (1.0 / dh)
        g = g_ref[rows[c], :]
        y_ref[rows[c], :] = o * lax.rsqrt(ms + EPS) * gn * (g * _sigmoid(g))


def _retention(z, theta, ret_norm, n_batch, t_len, state=None):
    chained = state is not None
    n_chunks = t_len // RET_BLOCK if chained else 4
    rows = n_chunks * RET_BLOCK
    n_steps = n_batch * t_len // rows
    blk = lambda c0: pl.BlockSpec((rows, LANES), lambda b, p: (b, c0 // LANES + p))
    in_specs = [pl.BlockSpec((1, 2, LANES), lambda b, p: (p, 0, 0)),
                blk(COL_RT_Q), blk(COL_RT_K), blk(COL_RT_V), blk(COL_RT_G),
                pl.BlockSpec((1, LANES), lambda b, p: (0, p))]
    args = [theta, z, z, z, z, ret_norm.reshape(1, RET_HEADS * RET_DH)]
    y_spec = pl.BlockSpec((rows, LANES), lambda b, p: (b, p))
    y_shape = jax.ShapeDtypeStruct((n_batch * t_len, BRANCH_W), F32)
    if chained:
        sblk = pl.BlockSpec((1, 2, RET_DH, RET_DH), lambda b, p: (b, p, 0, 0))
        in_specs += [sblk, sblk]
        args += list(state)
        out_specs, out_shape = y_spec, y_shape
    else:
        assert t_len == RET_BLOCK
        sblk = pl.BlockSpec((n_chunks, 2, RET_DH, RET_DH), lambda b, p: (b, p, 0, 0))
        s_shape = jax.ShapeDtypeStruct((n_batch, RET_HEADS, RET_DH, RET_DH), F32)
        out_specs, out_shape = [y_spec, sblk, sblk], [y_shape, s_shape, s_shape]
    return pl.pallas_call(
        functools.partial(_ret_kernel, n_chunks=n_chunks, chained=chained),
        grid=(n_steps, RET_HEADS // 2),
        in_specs=in_specs, out_specs=out_specs, out_shape=out_shape,
        compiler_params=_params("parallel", "parallel"),
        name="retention_lat" if chained else "retention_ctx",
    )(*args)


def _shift_rows(x, k, fill, up):
    t_len = x.shape[0]
    if k % SUBLANES == 0:
        pad = jnp.full((k, x.shape[1]), fill, x.dtype)
        return jnp.concatenate([x[k:], pad], axis=0) if up else jnp.concatenate([pad, x[:t_len - k]], axis=0)
    row = lax.broadcasted_iota(jnp.int32, x.shape, 0)
    if up:
        return jnp.where(row < t_len - k, pltpu.roll(x, t_len - k, axis=0), fill)
    return jnp.where(row >= k, pltpu.roll(x, k, axis=0), fill)


def _linear_scan(a, u, up):
    t_len = a.shape[0]
    k = 1
    while k < t_len:
        u = a * _shift_rows(u, k, 0.0, up) + u
        if 2 * k < t_len:
            a = a * _shift_rows(a, k, 1.0, up)
        k *= 2
    return u


def _lru_kernel(*refs, t_len, has_state, write_state):
    x_ref, g_ref, vec_ref, w_ref = refs[:4]
    for s in range(x_ref.shape[0] // t_len):
        _lru_sequence(s, slice(s * t_len, (s + 1) * t_len), refs, has_state, write_state)


def _lru_sequence(s, rows, refs, has_state, write_state):
    x_ref, g_ref, vec_ref, w_ref = refs[:4]
    refs = refs[4:]
    if has_state:
        h0f_ref, h0b_ref = refs[:2]
        refs = refs[2:]
    y_ref = refs[0]
    if write_state:
        hf_ref, hb_ref = refs[1:3]
    x = x_ref[rows, :]
    t_len = x.shape[0]
    row = lax.broadcasted_iota(jnp.int32, x.shape, 0)
    vec = lambda i: vec_ref[i:i + 1, :]
    xd = (vec(0) * _shift_rows(x, 1, 0.0, False) + vec(1) * x + vec(2) * _shift_rows(x, 1, 0.0, True)
          + vec(3) * _shift_rows(x, 2, 0.0, True) + vec(4))

    def gates(wa, wx, ba, bx, lam):
        r = _sigmoid(_dot(xd, wa) + ba)
        i = _sigmoid(_dot(xd, wx) + bx)
        nl = -lam
        softplus = jnp.maximum(nl, 0.0) + jnp.log1p(jnp.exp(-jnp.abs(nl)))
        log_a = -LRU_C * r * softplus
        a = jnp.exp(log_a)
        return a, jnp.sqrt(-jnp.tanh(log_a) * (a * a + 1.0)) * (i * xd)

    a, u = gates(w_ref[0, 0], w_ref[1, 0], vec(5), vec(6), vec(7))
    if has_state:
        u = u + jnp.where(row == 0, a * h0f_ref[s], 0.0)
    h_f = _linear_scan(a, u, False)
    a, u = gates(w_ref[2, 0], w_ref[3, 0], vec(8), vec(9), vec(10))
    if has_state:
        u = u + jnp.where(row == t_len - 1, a * h0b_ref[s], 0.0)
    h_b = _linear_scan(a, u, True)
    g = g_ref[rows, :]
    gelu = 0.5 * g * (1.0 + jnp.tanh(math.sqrt(2.0 / math.pi) * (g + 0.044715 * (g * g * g))))
    y_ref[rows, :] = (h_f + h_b) * gelu
    if write_state:
        hf_ref[s] = h_f[t_len - 1:t_len, :]
        hb_ref[s] = h_b[0:1, :]


def _rglru(z, vecs, w_gate, n_batch, t_len, state=None):
    n_seq = 1 if state is not None else 4
    rows = n_seq * t_len
    blk = lambda c0: pl.BlockSpec((rows, LANES), lambda b, j: (b, c0 // LANES + j))
    hblk = pl.BlockSpec((n_seq, 1, LANES), lambda b, j: (b, 0, j))
    in_specs = [blk(COL_LR_X), blk(COL_LR_G),
                pl.BlockSpec((11, LANES), lambda b, j: (0, j)),
                pl.BlockSpec((4, 1, LANES, LANES), lambda b, j: (0, j, 0, 0))]
    args = [z, z, vecs, w_gate]
    y_spec = pl.BlockSpec((rows, LANES), lambda b, j: (b, j))
    y_shape = jax.ShapeDtypeStruct((n_batch * t_len, BRANCH_W), F32)
    if state is not None:
        in_specs += [hblk, hblk]
        args += [s.reshape(n_batch, 1, LRU_WIDTH) for s in state]
        out_specs, out_shape = y_spec, y_shape
    else:
        h_shape = jax.ShapeDtypeStruct((n_batch, 1, LRU_WIDTH), F32)
        out_specs, out_shape = [y_spec, hblk, hblk], [y_shape, h_shape, h_shape]
    return pl.pallas_call(
        functools.partial(_lru_kernel, t_len=t_len, has_state=state is not None, write_state=state is None),
        grid=(n_batch // n_seq, LRU_WIDTH // LANES),
        in_specs=in_specs, out_specs=out_specs, out_shape=out_shape,
        compiler_params=_params("parallel", "parallel"),
        name="rglru_lat" if state is not None else "rglru_ctx",
    )(*args)


def _pair_block_diag(w):
    bw = LRU_WIDTH // LRU_BLOCKS
    w = w.reshape(LRU_BLOCKS // 2, 2, bw, bw)
    zero = jnp.zeros_like(w[:, 0])
    top = jnp.concatenate([w[:, 0], zero], axis=2)
    bot = jnp.concatenate([zero, w[:, 1]], axis=2)
    return jnp.concatenate([top, bot], axis=1)


def _merge_kernel(x_ref, ya_ref, yb_ref, yc_ref, yd_ref, wg01_ref, wg23_ref, wb_ref, wo_ref, gpre_ref, gpost_ref,
                  m_ref, o_ref):
    x = x_ref[...]
    h = (_rms(x, gpre_ref[...]) * (1.0 + m_ref[0, 1:2, :]) + m_ref[0, 0:1, :]).astype(BF16)
    acc = None
    for k, y_ref in enumerate((ya_ref, yb_ref, yc_ref, yd_ref)):
        wg_ref = wg01_ref if k < 2 else wg23_ref
        logits = jnp.dot(h, wg_ref[0, :, (k % 2) * D_MODEL:(k % 2 + 1) * D_MODEL], preferred_element_type=F32)
        term = _sigmoid(logits) * jnp.dot(y_ref[...].astype(BF16), wb_ref[0, k], preferred_element_type=F32)
        acc = term if acc is None else acc + term
    out = jnp.dot(acc.astype(BF16), wo_ref[0], preferred_element_type=F32)
    o_ref[...] = x + m_ref[0, 2:3, :] * _rms(out, gpost_ref[...])


def _merge(x, ys, mod, w_in_bf, wb_bf, wo_bf, gpre, gpost, l, mod_row):
    n_tok = x.shape[0]
    tm = 256
    yblk = pl.BlockSpec((tm, BRANCH_W), lambda i: (i, 0))
    return pl.pallas_call(
        _merge_kernel,
        grid=(n_tok // tm,),
        in_specs=[pl.BlockSpec((tm, D_MODEL), lambda i: (i, 0)), yblk, yblk, yblk, yblk,
                  pl.BlockSpec((1, D_MODEL, 2 * D_MODEL), lambda i: (l, 0, COL_GATE // (2 * D_MODEL))),
                  pl.BlockSpec((1, D_MODEL, 2 * D_MODEL), lambda i: (l, 0, COL_GATE // (2 * D_MODEL) + 1)),
                  pl.BlockSpec((1, N_BRANCH, BRANCH_W, D_MODEL), lambda i: (l, 0, 0, 0)),
                  pl.BlockSpec((1, D_MODEL, D_MODEL), lambda i: (l, 0, 0)),
                  pl.BlockSpec((1, D_MODEL), lambda i: (0, 0)),
                  pl.BlockSpec((1, D_MODEL), lambda i: (0, 0)),
                  pl.BlockSpec((1, 6, D_MODEL), lambda i: (mod_row(i, tm), 0, 0))],
        out_specs=pl.BlockSpec((tm, D_MODEL), lambda i: (i, 0)),
        out_shape=jax.ShapeDtypeStruct((n_tok, D_MODEL), F32),
        compiler_params=_params("parallel"),
        name="merge",
    )(x, *ys, w_in_bf, w_in_bf, wb_bf, wo_bf, gpre.reshape(1, D_MODEL), gpost.reshape(1, D_MODEL), mod)


def _mlp_kernel(x_ref, m_ref, gpre_ref, gpost_ref, w1_ref, w2_ref, o_ref):
    x = x_ref[...]
    h = (_rms(x, gpre_ref[...]) * (1.0 + m_ref[0, 4:5, :]) + m_ref[0, 3:4, :]).astype(BF16)
    ff_chunk = D_MODEL
    y = None
    for j in range(D_FF // ff_chunk):
        sl = slice(j * ff_chunk, (j + 1) * ff_chunk)
        a = jnp.maximum(jnp.dot(h, w1_ref[0, :, sl], preferred_element_type=F32), 0.0)
        part = jnp.dot((a * a).astype(BF16), w2_ref[0, sl, :], preferred_element_type=F32)
        y = part if y is None else y + part
    o_ref[...] = x + m_ref[0, 5:6, :] * _rms(y, gpost_ref[...])


def _mlp(x, mod, gpre, gpost, w1_bf, w2_bf, l, mod_row):
    n_tok = x.shape[0]
    tm = 256
    return pl.pallas_call(
        _mlp_kernel,
        grid=(n_tok // tm,),
        in_specs=[pl.BlockSpec((tm, D_MODEL), lambda i: (i, 0)),
                  pl.BlockSpec((1, 6, D_MODEL), lambda i: (mod_row(i, tm), 0, 0)),
                  pl.BlockSpec((1, D_MODEL), lambda i: (0, 0)),
                  pl.BlockSpec((1, D_MODEL), lambda i: (0, 0)),
                  pl.BlockSpec((1, D_MODEL, D_FF), lambda i: (l, 0, 0)),
                  pl.BlockSpec((1, D_FF, D_MODEL), lambda i: (l, 0, 0))],
        out_specs=pl.BlockSpec((tm, D_MODEL), lambda i: (i, 0)),
        out_shape=jax.ShapeDtypeStruct((n_tok, D_MODEL), F32),
        compiler_params=_params("parallel"),
        name="mlp",
    )(x, mod, gpre.reshape(1, D_MODEL), gpost.reshape(1, D_MODEL), w1_bf, w2_bf)


def _cache_heads_kernel(*refs, n_heads, dh, transposed):
    o_ref = refs[-1]
    for l, z_ref in enumerate(refs[:-1]):
        if transposed:
            for p in range(n_heads * dh // LANES):
                zt = z_ref[:, p * LANES:(p + 1) * LANES].T
                for i in range(LANES // dh):
                    o_ref[0, l, p * (LANES // dh) + i] = zt[i * dh:(i + 1) * dh, :]
        else:
            for h in range(n_heads):
                o_ref[0, l, h] = z_ref[:, h * dh:(h + 1) * dh]


def _cache_heads(zs, col, n_heads, dh):
    transposed = dh < LANES
    tail = (dh, SEQ) if transposed else (SEQ, dh)
    out = pl.pallas_call(
        functools.partial(_cache_heads_kernel, n_heads=n_heads, dh=dh, transposed=transposed),
        grid=(BATCH,),
        in_specs=[pl.BlockSpec((SEQ, 512), lambda b: (b, col // 512))] * DEPTH,
        out_specs=pl.BlockSpec((1, DEPTH, n_heads) + tail, lambda b: (b, 0, 0, 0, 0)),
        out_shape=jax.ShapeDtypeStruct((BATCH, DEPTH, n_heads) + tail, F32),
        compiler_params=_params("parallel"),
        name="cache_heads",
    )(*zs)
    return jnp.swapaxes(out, -1, -2) if transposed else out


def _ctx_mod_row(i, tm):
    return 0


def _lat_mod_row(i, tm):
    return 1 + i // (DEC_SEQ // tm)


def kernel(x_prompt, x_sample, cache_na_k, cache_na_v, cache_diff_k, cache_diff_v, state_ret_fwd, state_ret_bwd, state_lru_fwd, state_lru_bwd, c, c_ctx, ada_w, ada_b, norm_mix_pre, norm_mix_post, norm_ffn_pre, norm_ffn_post, w_in, na_rpb, diff_lq1, diff_lk1, diff_lq2, diff_lk2, diff_norm, ret_theta_fwd, ret_theta_bwd, ret_norm, lru_conv_w, lru_conv_b, lru_wa_fwd, lru_ba_fwd, lru_wx_fwd, lru_bx_fwd, lru_lam_fwd, lru_wa_bwd, lru_ba_bwd, lru_wx_bwd, lru_bx_bwd, lru_lam_bwd, w_branch, w_out, mlp_w1, mlp_w2):
    xc = x_prompt.reshape(BATCH * SEQ, D_MODEL)
    xl = x_sample.reshape(DEC_BATCH * DEC_SEQ, D_MODEL)
    cond = jnp.concatenate([c_ctx[None, :], c, jnp.zeros((COND_PAD - N_COND, D_MODEL), F32)], axis=0)
    mods = _modulation(cond.T, ada_w, ada_b).reshape(DEPTH, COND_PAD, 6, D_MODEL)
    rope_tables = _rope_tables()
    pair_tables = _na_pair_tables(_na_col_tables(na_rpb))
    w_in_bf = w_in.astype(BF16)
    wb_bf = w_branch.astype(BF16)
    wo_bf = w_out.astype(BF16)
    w1_bf = mlp_w1.astype(BF16)
    w2_bf = mlp_w2.astype(BF16)
    na_kt = jnp.swapaxes(cache_na_k, -1, -2)
    na_vt = jnp.swapaxes(cache_na_v, -1, -2)
    diff_kt = jnp.swapaxes(cache_diff_k, -1, -2)
    outs = [[] for _ in range(4)]
    zcs = []
    for l in range(DEPTH):
        mod = mods[l]
        lam_init = 0.8 - 0.6 * math.exp(-0.3 * l)
        lam_p = jnp.stack([diff_lq1[l], diff_lk1[l], diff_lq2[l], diff_lk2[l]])
        theta = jnp.stack([jnp.repeat(t, RET_DH).reshape(RET_HEADS // 2, LANES)
                           for t in (ret_theta_fwd[l], ret_theta_bwd[l])], axis=1)
        vecs = jnp.concatenate([lru_conv_w[l], lru_conv_b[l][None],
                                lru_ba_fwd[l][None], lru_bx_fwd[l][None], lru_lam_fwd[l][None],
                                lru_ba_bwd[l][None], lru_bx_bwd[l][None], lru_lam_bwd[l][None]], axis=0)
        w_gate = jnp.stack([_pair_block_diag(w) for w in
                            (lru_wa_fwd[l], lru_wx_fwd[l], lru_wa_bwd[l], lru_wx_bwd[l])])

        zc = _inproj(xc, mod, norm_mix_pre[l], w_in_bf, l, _ctx_mod_row)
        ya = _ctx_na(zc)
        yb = _ctx_diff(zc, lam_p, diff_norm[l], lam_init)
        yc, s_rf, s_rb = _retention(zc, theta, ret_norm[l], BATCH, SEQ)
        yd, h_lf, h_lb = _rglru(zc, vecs, w_gate, BATCH, SEQ)
        xc = _merge(xc, (ya, yb, yc, yd), mod, w_in_bf, wb_bf, wo_bf, norm_mix_pre[l], norm_mix_post[l], l,
                    _ctx_mod_row)
        xc = _mlp(xc, mod, norm_ffn_pre[l], norm_ffn_post[l], w1_bf, w2_bf, l, _ctx_mod_row)
        zcs.append(zc)
        outs[0].append(s_rf)
        outs[1].append(s_rb)
        outs[2].append(h_lf.reshape(BATCH, LRU_WIDTH))
        outs[3].append(h_lb.reshape(BATCH, LRU_WIDTH))

        zl = _inproj(xl, mod, norm_mix_pre[l], w_in_bf, l, _lat_mod_row)
        ya = _lat_na(zl, na_kt, na_vt, pair_tables, l)
        yb = _lat_diff(zl, diff_kt, cache_diff_v, lam_p, diff_norm[l], rope_tables, lam_init, l)
        yc = _retention(zl, theta, ret_norm[l], DEC_BATCH, DEC_SEQ,
                        state=(state_ret_fwd[:, l], state_ret_bwd[:, l]))
        yd = _rglru(zl, vecs, w_gate, DEC_BATCH, DEC_SEQ, state=(state_lru_fwd[:, l], state_lru_bwd[:, l]))
        xl = _merge(xl, (ya, yb, yc, yd), mod, w_in_bf, wb_bf, wo_bf, norm_mix_pre[l], norm_mix_post[l], l,
                    _lat_mod_row)
        xl = _mlp(xl, mod, norm_ffn_pre[l], norm_ffn_post[l], w1_bf, w2_bf, l, _lat_mod_row)

    new_na_k = _cache_heads(zcs, COL_NA_K, NA_HEADS, NA_DH)
    new_na_v = _cache_heads(zcs, COL_NA_V, NA_HEADS, NA_DH)
    new_diff_k = _cache_heads(zcs, COL_DF_K, 2 * DIFF_HEADS, DIFF_DH).reshape(
        BATCH, DEPTH, 2, DIFF_HEADS, SEQ, DIFF_DH)
    new_diff_v = _cache_heads(zcs, COL_DF_V, DIFF_HEADS, 2 * DIFF_DH)
    states = [jnp.stack(o, axis=1) for o in outs]
    return (xc.reshape(BATCH, SEQ, D_MODEL), xl.reshape(DEC_BATCH, DEC_SEQ, D_MODEL),
            new_na_k, new_na_v, new_diff_k, new_diff_v, *states)
```

```python
import functools
import math

import numpy as np
import jax
import jax.numpy as jnp
from jax import lax
from jax.experimental import pallas as pl
from jax.experimental.pallas import tpu as pltpu

F32 = jnp.float32
BF16 = jnp.bfloat16

D_MODEL = 1024
BATCH = 32
SEQ = 256
DEPTH = 4
DEC_BATCH = 4
DEC_SEQ = 2048
PAST_LEN = 512
GRID_W = 64
NA_HEADS = 8
NA_DH = 64
NA_ROWS = 8
NA_COLS = 16
DIFF_HEADS = 4
DIFF_DH = 64
RET_HEADS = 8
RET_DH = 64
LRU_WIDTH = 512
LRU_BLOCKS = 8
LRU_CONV = 4
LRU_C = 8.0
N_BRANCH = 4
BRANCH_W = 512
D_FF = 4 * D_MODEL
ROPE_BASE = 10000.0
EPS = 1e-6
NEG_INF = -1e30

COL_NA_Q, COL_NA_K, COL_NA_V = 0, 512, 1024
COL_DF_Q, COL_DF_K, COL_DF_V = 1536, 2048, 2560
COL_RT_Q, COL_RT_K, COL_RT_V, COL_RT_G = 3072, 3584, 4096, 4608
COL_LR_X, COL_LR_G = 5120, 5632
COL_GATE = 6144
Z_WIDTH = COL_GATE

N_COND = 1 + DEC_BATCH
COND_PAD = 8
LANES = 128
SUBLANES = 8
VMEM_LIMIT = 56 * 1024 * 1024


def _params(*sem):
    return pltpu.CompilerParams(dimension_semantics=sem, vmem_limit_bytes=VMEM_LIMIT)


def _dot(a, b):
    return jnp.dot(a.astype(BF16), b.astype(BF16), preferred_element_type=F32)


def _dot_nt(a, b):
    return lax.dot_general(a.astype(BF16), b.astype(BF16), (((1,), (1,)), ((), ())),
                           preferred_element_type=F32)


def _dot_tn(a, b):
    return lax.dot_general(a.astype(BF16), b.astype(BF16), (((0,), (0,)), ((), ())),
                           preferred_element_type=F32)


def _rms(x, g):
    return x * lax.rsqrt(jnp.mean(x * x, axis=-1, keepdims=True) + EPS) * g


def _sigmoid(x):
    return 1.0 / (1.0 + jnp.exp(-x))


LOG2E = math.log2(math.e)


def _exp2_weights(parts):
    m = functools.reduce(jnp.maximum, [jnp.max(s, axis=-1, keepdims=True) for s in parts])
    return [jnp.exp2(s - m).astype(BF16) for s in parts]


def _with_ones(v):
    return jnp.concatenate([v.astype(BF16), jnp.ones(v.shape, BF16)], axis=1)


def _normalised(r):
    w = r.shape[1] // 2
    return r[:, :w] / r[:, w:]


def _mod_kernel(ct_ref, w_ref, b_ref, o_ref):
    ct = ct_ref[...]
    s = ct * _sigmoid(ct)
    w = w_ref[0]
    rows = [jnp.sum(s[:, r:r + 1] * w, axis=0, keepdims=True) + b_ref[0] for r in range(N_COND)]
    rows.append(jnp.zeros((COND_PAD - N_COND, w.shape[1]), F32))
    o_ref[0] = jnp.concatenate(rows, axis=0)


def _modulation(cond_t, ada_w, ada_b):
    tn = 768
    n = 6 * D_MODEL
    return pl.pallas_call(
        _mod_kernel,
        grid=(DEPTH, n // tn),
        in_specs=[pl.BlockSpec((D_MODEL, COND_PAD), lambda l, j: (0, 0)),
                  pl.BlockSpec((1, D_MODEL, tn), lambda l, j: (l, 0, j)),
                  pl.BlockSpec((1, 1, tn), lambda l, j: (l, 0, j))],
        out_specs=pl.BlockSpec((1, COND_PAD, tn), lambda l, j: (l, 0, j)),
        out_shape=jax.ShapeDtypeStruct((DEPTH, COND_PAD, n), F32),
        compiler_params=_params("parallel", "parallel"),
        name="modulation",
    )(cond_t, ada_w, ada_b.reshape(DEPTH, 1, n))


def _inproj_kernel(x_ref, m_ref, g_ref, w_ref, z_ref, h_scr):
    @pl.when(pl.program_id(1) == 0)
    def _():
        h = _rms(x_ref[...], g_ref[...]) * (1.0 + m_ref[0, 1:2, :]) + m_ref[0, 0:1, :]
        h_scr[...] = h.astype(BF16)

    z_ref[...] = jnp.dot(h_scr[...], w_ref[0], preferred_element_type=F32)


def _inproj(x, mod, gain, w_bf, l, mod_row):
    n_tok = x.shape[0]
    tm, tn = 2048, 1024
    return pl.pallas_call(
        _inproj_kernel,
        grid=(n_tok // tm, Z_WIDTH // tn),
        in_specs=[pl.BlockSpec((tm, D_MODEL), lambda i, j: (i, 0)),
                  pl.BlockSpec((1, 6, D_MODEL), lambda i, j: (mod_row(i, tm), 0, 0)),
                  pl.BlockSpec((1, D_MODEL), lambda i, j: (0, 0)),
                  pl.BlockSpec((1, D_MODEL, tn), lambda i, j: (l, 0, j))],
        out_specs=pl.BlockSpec((tm, tn), lambda i, j: (i, j)),
        out_shape=jax.ShapeDtypeStruct((n_tok, Z_WIDTH), F32),
        scratch_shapes=[pltpu.VMEM((tm, D_MODEL), BF16)],
        compiler_params=_params("parallel", "arbitrary"),
        name="inproj",
    )(x, mod, gain.reshape(1, D_MODEL), w_bf)


def _ctx_na_kernel(q_ref, k_ref, v_ref, o_ref):
    qs = NA_DH ** -0.5 * LOG2E
    first = lax.broadcasted_iota(jnp.int32, (SEQ, LANES), 1) < NA_DH
    for b in range(q_ref.shape[0] // SEQ):
        rows = slice(b * SEQ, (b + 1) * SEQ)
        for p in range(NA_HEADS * NA_DH // LANES):
            sl = slice(p * LANES, (p + 1) * LANES)
            q = q_ref[rows, sl] * qs
            k = k_ref[rows, sl].astype(BF16)
            vx = _with_ones(v_ref[rows, sl])
            att = []
            for keep in (first, ~first):
                (e,) = _exp2_weights([_dot_nt(jnp.where(keep, q, 0.0), k)])
                att.append(_normalised(jnp.dot(e, vx, preferred_element_type=F32)))
            o_ref[rows, sl] = jnp.where(first, att[0], att[1])


CTX_SEQS_PER_STEP = 2


def _ctx_na(z):
    rows = CTX_SEQS_PER_STEP * SEQ
    blk = lambda j: pl.BlockSpec((rows, 512), lambda b: (b, j))
    return pl.pallas_call(
        _ctx_na_kernel,
        grid=(BATCH // CTX_SEQS_PER_STEP,),
        in_specs=[blk(COL_NA_Q // 512), blk(COL_NA_K // 512), blk(COL_NA_V // 512)],
        out_specs=pl.BlockSpec((rows, 512), lambda b: (b, 0)),
        out_shape=jax.ShapeDtypeStruct((BATCH * SEQ, BRANCH_W), F32),
        compiler_params=_params("parallel"),
        name="ctx_na",
    )(z, z, z)


def _diff_lambda(lam_ref, lam_init):
    a = jnp.sum(lam_ref[0:1, :] * lam_ref[1:2, :], axis=-1, keepdims=True)
    b = jnp.sum(lam_ref[2:3, :] * lam_ref[3:4, :], axis=-1, keepdims=True)
    return jnp.exp(a) - jnp.exp(b) + lam_init


def _ctx_diff_kernel(lam_ref, q_ref, k_ref, v_ref, dn_ref, o_ref, *, lam_init):
    qs = DIFF_DH ** -0.5 * LOG2E
    lam = _diff_lambda(lam_ref, lam_init)
    w = 2 * DIFF_DH
    first = lax.broadcasted_iota(jnp.int32, (SEQ, LANES), 1) < DIFF_DH
    for b in range(q_ref.shape[0] // SEQ):
        rows = slice(b * SEQ, (b + 1) * SEQ)
        for h in range(DIFF_HEADS):
            vx = _with_ones(v_ref[rows, h * w:(h + 1) * w])
            keep = first if h % 2 == 0 else ~first
            att = []
            for c in range(2):
                c0 = (c * DIFF_HEADS + h - h % 2) * DIFF_DH
                sl = slice(c0, c0 + LANES)
                (e,) = _exp2_weights([_dot_nt(jnp.where(keep, q_ref[rows, sl] * qs, 0.0), k_ref[rows, sl])])
                att.append(_normalised(jnp.dot(e, vx, preferred_element_type=F32)))
            o = att[0] - lam * att[1]
            o_ref[rows, h * w:(h + 1) * w] = _rms(o, dn_ref[...]) * (1.0 - lam_init)


def _ctx_diff(z, lam_p, diff_norm, lam_init):
    rows = CTX_SEQS_PER_STEP * SEQ
    blk = lambda j: pl.BlockSpec((rows, 512), lambda b: (b, j))
    return pl.pallas_call(
        functools.partial(_ctx_diff_kernel, lam_init=lam_init),
        grid=(BATCH // CTX_SEQS_PER_STEP,),
        in_specs=[pl.BlockSpec((4, DIFF_DH), lambda b: (0, 0)),
                  blk(COL_DF_Q // 512), blk(COL_DF_K // 512), blk(COL_DF_V // 512),
                  pl.BlockSpec((1, 2 * DIFF_DH), lambda b: (0, 0))],
        out_specs=pl.BlockSpec((rows, 512), lambda b: (b, 0)),
        out_shape=jax.ShapeDtypeStruct((BATCH * SEQ, BRANCH_W), F32),
        compiler_params=_params("parallel"),
        name="ctx_diff",
    )(lam_p, z, z, z, diff_norm.reshape(1, 2 * DIFF_DH))


NA_GRID_ROWS = DEC_SEQ // GRID_W
NA_KEY_ROWS = min(NA_ROWS, NA_GRID_ROWS)


NA_DR = 2 * NA_ROWS - 1
NA_DC = 2 * NA_COLS - 1


def _na_col_kernel(rpb_ref, idx_ref, o_ref):
    idx = jnp.broadcast_to(idx_ref[...], o_ref.shape)
    acc = jnp.full(o_ref.shape, NEG_INF, F32)
    for d in range(NA_DC):
        acc = jnp.where(idx == d, jnp.broadcast_to(rpb_ref[:, d:d + 1], o_ref.shape), acc)
    o_ref[...] = acc * LOG2E


def _na_col_tables(na_rpb):
    qc = np.arange(GRID_W)[:, None]
    kc = np.arange(GRID_W)[None, :]
    cstart = np.clip(qc - NA_COLS // 2, 0, GRID_W - NA_COLS)
    ok = (kc >= cstart) & (kc < cstart + NA_COLS)
    idx = np.where(ok, kc - qc + NA_COLS - 1, -1).astype(np.int32).reshape(1, GRID_W * GRID_W)
    rows = NA_HEADS * NA_DR
    rpb = jnp.pad(na_rpb.reshape(DEPTH * rows, NA_DC), ((0, 0), (0, 32 - NA_DC)))
    t = pl.pallas_call(
        _na_col_kernel,
        grid=(DEPTH,),
        in_specs=[pl.BlockSpec((rows, 32), lambda l: (l, 0)),
                  pl.BlockSpec((1, GRID_W * GRID_W), lambda l: (0, 0))],
        out_specs=pl.BlockSpec((rows, GRID_W * GRID_W), lambda l: (l, 0)),
        out_shape=jax.ShapeDtypeStruct((DEPTH * rows, GRID_W * GRID_W), F32),
        compiler_params=_params("parallel"),
        name="na_col_tables",
    )(rpb, jnp.asarray(idx))
    return t.reshape(DEPTH, NA_HEADS, NA_DR, GRID_W, GRID_W)


NA_QROWS = 4
NA_UROWS = 12
NA_QBLK = NA_QROWS * GRID_W
NA_UKEYS = NA_UROWS * GRID_W
NA_NBLK = NA_GRID_ROWS // NA_QROWS


def _na_union_start(r0):
    return int(np.clip(r0 - NA_KEY_ROWS // 2, 0, NA_GRID_ROWS - NA_UROWS))


def _na_pair_tables(t):
    tp = jnp.pad(t, ((0, 0), (0, 0), (1, 1), (0, 0), (0, 0)), constant_values=NEG_INF)
    return jnp.concatenate([tp[:, :, :-1], tp[:, :, 1:]], axis=-1)


def _na_fill_bias(pair_ref, bias_scr):
    first = lax.broadcasted_iota(jnp.int32, (GRID_W, LANES), 1) < GRID_W
    neg = jnp.full((GRID_W, LANES), NEG_INF, F32)
    for variant, r0 in enumerate((0, NA_QROWS, NA_GRID_ROWS - NA_QROWS)):
        us = _na_union_start(r0)
        for j in range(NA_QROWS):
            r = r0 + j
            rs = int(np.clip(r - NA_KEY_ROWS // 2, 0, NA_GRID_ROWS - NA_KEY_ROWS))
            for m in range(NA_UROWS // 2):
                ok = [rs <= us + i < rs + NA_KEY_ROWS for i in (2 * m, 2 * m + 1)]
                p = us + 2 * m - r + NA_ROWS
                for hh in range(2):
                    if ok[0] and ok[1]:
                        tile = pair_ref[0, hh, p]
                    elif ok[0]:
                        tile = jnp.where(first, pair_ref[0, hh, p], NEG_INF)
                    elif ok[1]:
                        tile = jnp.where(first, NEG_INF, pair_ref[0, hh, p])
                    else:
                        tile = neg
                    bias_scr[variant, hh, j * GRID_W:(j + 1) * GRID_W, m * LANES:(m + 1) * LANES] = tile


def _lat_na_kernel(q_ref, k_ref, v_ref, kc_ref, vc_ref, pair_ref, o_ref, k_scr, vx_scr, kc_scr, vcx_scr, bias_ref):
    qs = NA_DH ** -0.5 * LOG2E
    _na_fill_bias(pair_ref, bias_ref)
    k_scr[...] = k_ref[...].astype(BF16)
    vx_scr[...] = _with_ones(v_ref[...])
    kc_scr[...] = jnp.concatenate([kc_ref[0, 0, 0], kc_ref[0, 0, 1]], axis=0).astype(BF16)
    vcx_scr[...] = jnp.concatenate([vc_ref[0, 0, 0].astype(BF16), vc_ref[0, 0, 1].astype(BF16),
                                    jnp.ones((LANES, PAST_LEN), BF16)], axis=0)
    first = lax.broadcasted_iota(jnp.int32, (NA_QBLK, LANES), 1) < NA_DH

    for blk in range(NA_NBLK):
        us = _na_union_start(blk * NA_QROWS)
        variant = 0 if blk == 0 else (2 if blk == NA_NBLK - 1 else 1)
        rows = slice(blk * NA_QBLK, (blk + 1) * NA_QBLK)
        keys = slice(us * GRID_W, us * GRID_W + NA_UKEYS)
        q = q_ref[rows, :] * qs
        att = []
        for hh, keep in enumerate((first, ~first)):
            qm = jnp.where(keep, q, 0.0).astype(BF16)
            s_loc = _dot_nt(qm, k_scr[keys, :]) + bias_ref[variant, hh]
            s_ctx = jnp.dot(qm, kc_scr[...], preferred_element_type=F32)
            e_loc, e_ctx = _exp2_weights([s_loc, s_ctx])
            att.append(_normalised(jnp.dot(e_loc, vx_scr[keys, :], preferred_element_type=F32)
                                   + _dot_nt(e_ctx, vcx_scr[...])))
        o_ref[rows, :] = jnp.where(first, att[0], att[1])


def _lat_na(z, cache_k, cache_v, pair_tables, l):
    blk = lambda c0: pl.BlockSpec((DEC_SEQ, LANES), lambda p, b: (b, c0 // LANES + p))
    cblk = pl.BlockSpec((1, 1, 2, NA_DH, PAST_LEN), lambda p, b: (b, l, p, 0, 0))
    return pl.pallas_call(
        _lat_na_kernel,
        grid=(NA_HEADS // 2, DEC_BATCH),
        in_specs=[blk(COL_NA_Q), blk(COL_NA_K), blk(COL_NA_V), cblk, cblk,
                  pl.BlockSpec((1, 2, NA_DR + 1, GRID_W, LANES), lambda p, b: (l, p, 0, 0, 0))],
        out_specs=pl.BlockSpec((DEC_SEQ, LANES), lambda p, b: (b, p)),
        out_shape=jax.ShapeDtypeStruct((DEC_BATCH * DEC_SEQ, BRANCH_W), F32),
        scratch_shapes=[pltpu.VMEM((DEC_SEQ, LANES), BF16), pltpu.VMEM((DEC_SEQ, 2 * LANES), BF16),
                        pltpu.VMEM((LANES, PAST_LEN), BF16), pltpu.VMEM((2 * LANES, PAST_LEN), BF16),
                        pltpu.VMEM((3, 2, NA_QBLK, NA_UKEYS), F32)],
        compiler_params=_params("parallel", "parallel"),
        name="lat_na",
    )(z, z, z, cache_k, cache_v, pair_tables)


def _rope_tables():
    t = jnp.arange(DEC_SEQ)
    quarter = DIFF_DH // 4
    inv = ROPE_BASE ** (-jnp.arange(quarter, dtype=F32) / quarter)
    ang_r = (t // GRID_W).astype(F32)[:, None] * inv
    ang_c = (t % GRID_W).astype(F32)[:, None] * inv
    lane = np.arange(LANES) % DIFF_DH
    idx = lane % quarter
    use_r = jnp.asarray(lane < DIFF_DH // 2)[None, :]
    first = jnp.asarray((lane % (2 * quarter)) < quarter)[None, :]
    ang = jnp.where(use_r, ang_r[:, idx], ang_c[:, idx])
    cos = jnp.cos(ang)
    sin = jnp.sin(ang)
    return cos, jnp.where(first, -sin, 0.0), jnp.where(first, 0.0, sin)


def _rope(x, cos, sin_a, sin_b):
    quarter = DIFF_DH // 4
    up = pltpu.roll(x, LANES - quarter, axis=1)
    down = pltpu.roll(x, quarter, axis=1)
    return x * cos + up * sin_a + down * sin_b


def _lat_diff_kernel(lam_ref, q_ref, k_ref, v_ref, kc_ref, vc_ref, cos_ref, sa_ref, sb_ref, dn_ref, o_ref,
                     k_scr, vx_scr, *, lam_init, tq):
    qs = DIFF_DH ** -0.5 * LOG2E
    n_slab = 512 // LANES
    w = 2 * DIFF_DH
    lat = slice(0, DEC_SEQ)
    ctx = slice(DEC_SEQ, DEC_SEQ + PAST_LEN)

    @pl.when(pl.program_id(1) == 0)
    def _():
        for j in range(n_slab):
            k_scr[j, lat, :] = _rope(k_ref[:, j * LANES:(j + 1) * LANES],
                                     cos_ref[...], sa_ref[...], sb_ref[...]).astype(BF16)
        for c in range(2):
            for hp in range(DIFF_HEADS // 2):
                k_scr[c * (DIFF_HEADS // 2) + hp, ctx, :] = jnp.concatenate(
                    [kc_ref[0, 0, c, 2 * hp], kc_ref[0, 0, c, 2 * hp + 1]], axis=0).T.astype(BF16)
        for h in range(DIFF_HEADS):
            vx_scr[h, lat, :] = _with_ones(v_ref[:, h * w:(h + 1) * w])
            vx_scr[h, ctx, :] = _with_ones(vc_ref[0, 0, h])

    r0 = pl.multiple_of(pl.program_id(1) * tq, tq)
    cos, sa, sb = cos_ref[pl.ds(r0, tq), :], sa_ref[pl.ds(r0, tq), :], sb_ref[pl.ds(r0, tq), :]
    qrot = [_rope(q_ref[:, j * LANES:(j + 1) * LANES], cos, sa, sb) * qs for j in range(n_slab)]
    lam = _diff_lambda(lam_ref, lam_init)
    first = lax.broadcasted_iota(jnp.int32, (tq, LANES), 1) < DIFF_DH
    for h in range(DIFF_HEADS):
        keep = first if h % 2 == 0 else ~first
        att = []
        for c in range(2):
            j = c * (DIFF_HEADS // 2) + h // 2
            (e,) = _exp2_weights([_dot_nt(jnp.where(keep, qrot[j], 0.0), k_scr[j])])
            att.append(_normalised(jnp.dot(e, vx_scr[h], preferred_element_type=F32)))
        o = att[0] - lam * att[1]
        o_ref[:, h * w:(h + 1) * w] = _rms(o, dn_ref[...]) * (1.0 - lam_init)


def _lat_diff(z, cache_k, cache_v, lam_p, diff_norm, tables, lam_init, l):
    tq = 256
    nq = DEC_SEQ // tq
    tab = pl.BlockSpec((DEC_SEQ, LANES), lambda b, i: (0, 0))
    return pl.pallas_call(
        functools.partial(_lat_diff_kernel, lam_init=lam_init, tq=tq),
        grid=(DEC_BATCH, nq),
        in_specs=[pl.BlockSpec((4, DIFF_DH), lambda b, i: (0, 0)),
                  pl.BlockSpec((tq, 512), lambda b, i: (b * nq + i, COL_DF_Q // 512)),
                  pl.BlockSpec((DEC_SEQ, 512), lambda b, i: (b, COL_DF_K // 512)),
                  pl.BlockSpec((DEC_SEQ, 512), lambda b, i: (b, COL_DF_V // 512)),
                  pl.BlockSpec((1, 1, 2, DIFF_HEADS, DIFF_DH, PAST_LEN), lambda b, i: (b, l, 0, 0, 0, 0)),
                  pl.BlockSpec((1, 1, DIFF_HEADS, PAST_LEN, 2 * DIFF_DH), lambda b, i: (b, l, 0, 0, 0)),
                  tab, tab, tab,
                  pl.BlockSpec((1, 2 * DIFF_DH), lambda b, i: (0, 0))],
        out_specs=pl.BlockSpec((tq, 512), lambda b, i: (b * nq + i, 0)),
        out_shape=jax.ShapeDtypeStruct((DEC_BATCH * DEC_SEQ, BRANCH_W), F32),
        scratch_shapes=[pltpu.VMEM((DIFF_HEADS, DEC_SEQ + PAST_LEN, LANES), BF16),
                        pltpu.VMEM((DIFF_HEADS, DEC_SEQ + PAST_LEN, 4 * DIFF_DH), BF16)],
        compiler_params=_params("parallel", "arbitrary"),
        name="lat_diff",
    )(lam_p, z, z, z, cache_k, cache_v, *tables, diff_norm.reshape(1, 2 * DIFF_DH))


RET_BLOCK = 256


def _block_diag2(a, b):
    zero = jnp.zeros_like(a)
    return jnp.concatenate([jnp.concatenate([a, zero], axis=1), jnp.concatenate([zero, b], axis=1)], axis=0)


def _ret_kernel(*refs, n_chunks, chained):
    th_ref, q_ref, k_ref, v_ref, g_ref, gn_ref = refs[:6]
    if chained:
        s0f_ref, s0b_ref, y_ref = refs[6:]
    else:
        y_ref, sf_ref, sb_ref = refs[6:]
    C = RET_BLOCK
    dh = RET_DH
    rel = (lax.broadcasted_iota(jnp.int32, (C, C), 0) - lax.broadcasted_iota(jnp.int32, (C, C), 1)).astype(F32)
    pos = lax.broadcasted_iota(jnp.int32, (C, LANES), 0).astype(F32)
    first = lax.broadcasted_iota(jnp.int32, (C, LANES), 1) < dh
    lg = jnp.log1p(-jnp.exp(th_ref[0]))
    lgf, lgb = lg[0:1, :], lg[1:2, :]
    decays = [jnp.exp(jnp.where(rel >= 0, rel * lgf[:, c0:c0 + 1], -rel * lgb[:, c0:c0 + 1])) for c0 in (0, dh)]
    xi_f = jnp.exp((pos + 1.0) * lgf)
    zeta_f = jnp.exp((C - 1.0 - pos) * lgf)
    xi_b = jnp.exp((C - pos) * lgb)
    zeta_b = jnp.exp(pos * lgb)
    gn = gn_ref[...]
    rows = [pl.ds(c * C, C) for c in range(n_chunks)]
    ks = [k_ref[r, :] * (dh ** -0.5) for r in rows]
    vs = [v_ref[r, :] for r in rows]
    uf = [_dot_tn(k * zeta_f, v) for k, v in zip(ks, vs)]
    ub = [_dot_tn(k * zeta_b, v) for k, v in zip(ks, vs)]
    if chained:
        diag = ((lax.broadcasted_iota(jnp.int32, (LANES, LANES), 0) < dh)
                == (lax.broadcasted_iota(jnp.int32, (LANES, LANES), 1) < dh))
        gc_f = jnp.exp(C * lgf)
        gc_b = jnp.exp(C * lgb)
        s_f = [_block_diag2(s0f_ref[0, 0], s0f_ref[0, 1])]
        for c in range(n_chunks - 1):
            s_f.append(gc_f * s_f[c] + jnp.where(diag, uf[c], 0.0))
        s_b = [_block_diag2(s0b_ref[0, 0], s0b_ref[0, 1])]
        for c in range(n_chunks - 1, 0, -1):
            s_b.append(gc_b * s_b[-1] + jnp.where(diag, ub[c], 0.0))
        s_b = s_b[::-1]
    for c in range(n_chunks):
        q = q_ref[rows[c], :]
        kb = ks[c].astype(BF16)
        vb = vs[c].astype(BF16)
        o_heads = [jnp.dot((_dot_nt(jnp.where(keep, q, 0.0), kb) * decay).astype(BF16), vb,
                           preferred_element_type=F32) for keep, decay in zip((first, ~first), decays)]
        o = jnp.where(first, o_heads[0], o_heads[1])
        if chained:
            o = o + _dot(q, s_f[c]) * xi_f + _dot(q, s_b[c]) * xi_b
        else:
            for hh in range(2):
                sf_ref[c, hh] = uf[c][hh * dh:(hh + 1) * dh, hh * dh:(hh + 1) * dh]
                sb_ref[c, hh] = ub[c][hh * dh:(hh + 1) * dh, hh * dh:(hh + 1) * dh]
        sq = o * o
        ms = jnp.where(first, jnp.sum(jnp.where(first, sq, 0.0), axis=-1, keepdims=True),
                       jnp.sum(jnp.where(first, 0.0, sq), axis=-1, keepdims=True)) * (1.0 / dh)
        g = g_ref[rows[c], :]
        y_ref[rows[c], :] = o * lax.rsqrt(ms + EPS) * gn * (g * _sigmoid(g))


def _retention(z, theta, ret_norm, n_batch, t_len, state=None):
    chained = state is not None
    n_chunks = t_len // RET_BLOCK if chained else 4
    rows = n_chunks * RET_BLOCK
    n_steps = n_batch * t_len // rows
    blk = lambda c0: pl.BlockSpec((rows, LANES), lambda b, p: (b, c0 // LANES + p))
    in_specs = [pl.BlockSpec((1, 2, LANES), lambda b, p: (p, 0, 0)),
                blk(COL_RT_Q), blk(COL_RT_K), blk(COL_RT_V), blk(COL_RT_G),
                pl.BlockSpec((1, LANES), lambda b, p: (0, p))]
    args = [theta, z, z, z, z, ret_norm.reshape(1, RET_HEADS * RET_DH)]
    y_spec = pl.BlockSpec((rows, LANES), lambda b, p: (b, p))
    y_shape = jax.ShapeDtypeStruct((n_batch * t_len, BRANCH_W), F32)
    if chained:
        sblk = pl.BlockSpec((1, 2, RET_DH, RET_DH), lambda b, p: (b, p, 0, 0))
        in_specs += [sblk, sblk]
        args += list(state)
        out_specs, out_shape = y_spec, y_shape
    else:
        assert t_len == RET_BLOCK
        sblk = pl.BlockSpec((n_chunks, 2, RET_DH, RET_DH), lambda b, p: (b, p, 0, 0))
        s_shape = jax.ShapeDtypeStruct((n_batch, RET_HEADS, RET_DH, RET_DH), F32)
        out_specs, out_shape = [y_spec, sblk, sblk], [y_shape, s_shape, s_shape]
    return pl.pallas_call(
        functools.partial(_ret_kernel, n_chunks=n_chunks, chained=chained),
        grid=(n_steps, RET_HEADS // 2),
        in_specs=in_specs, out_specs=out_specs, out_shape=out_shape,
        compiler_params=_params("parallel", "parallel"),
        name="retention_lat" if chained else "retention_ctx",
    )(*args)


def _shift_rows(x, k, fill, up):
    t_len = x.shape[0]
    if k % SUBLANES == 0:
        pad = jnp.full((k, x.shape[1]), fill, x.dtype)
        return jnp.concatenate([x[k:], pad], axis=0) if up else jnp.concatenate([pad, x[:t_len - k]], axis=0)
    row = lax.broadcasted_iota(jnp.int32, x.shape, 0)
    if up:
        return jnp.where(row < t_len - k, pltpu.roll(x, t_len - k, axis=0), fill)
    return jnp.where(row >= k, pltpu.roll(x, k, axis=0), fill)


def _linear_scan(a, u, up):
    t_len = a.shape[0]
    k = 1
    while k < t_len:
        u = a * _shift_rows(u, k, 0.0, up) + u
        if 2 * k < t_len:
            a = a * _shift_rows(a, k, 1.0, up)
        k *= 2
    return u


def _lru_kernel(*refs, t_len, has_state, write_state):
    x_ref, g_ref, vec_ref, w_ref = refs[:4]
    for s in range(x_ref.shape[0] // t_len):
        _lru_sequence(s, slice(s * t_len, (s + 1) * t_len), refs, has_state, write_state)


def _lru_sequence(s, rows, refs, has_state, write_state):
    x_ref, g_ref, vec_ref, w_ref = refs[:4]
    refs = refs[4:]
    if has_state:
        h0f_ref, h0b_ref = refs[:2]
        refs = refs[2:]
    y_ref = refs[0]
    if write_state:
        hf_ref, hb_ref = refs[1:3]
    x = x_ref[rows, :]
    t_len = x.shape[0]
    row = lax.broadcasted_iota(jnp.int32, x.shape, 0)
    vec = lambda i: vec_ref[i:i + 1, :]
    xd = (vec(0) * _shift_rows(x, 1, 0.0, False) + vec(1) * x + vec(2) * _shift_rows(x, 1, 0.0, True)
          + vec(3) * _shift_rows(x, 2, 0.0, True) + vec(4))

    def gates(wa, wx, ba, bx, lam):
        r = _sigmoid(_dot(xd, wa) + ba)
        i = _sigmoid(_dot(xd, wx) + bx)
        nl = -lam
        softplus = jnp.maximum(nl, 0.0) + jnp.log1p(jnp.exp(-jnp.abs(nl)))
        log_a = -LRU_C * r * softplus
        a = jnp.exp(log_a)
        return a, jnp.sqrt(-jnp.tanh(log_a) * (a * a + 1.0)) * (i * xd)

    a, u = gates(w_ref[0, 0], w_ref[1, 0], vec(5), vec(6), vec(7))
    if has_state:
        u = u + jnp.where(row == 0, a * h0f_ref[s], 0.0)
    h_f = _linear_scan(a, u, False)
    a, u = gates(w_ref[2, 0], w_ref[3, 0], vec(8), vec(9), vec(10))
    if has_state:
        u = u + jnp.where(row == t_len - 1, a * h0b_ref[s], 0.0)
    h_b = _linear_scan(a, u, True)
    g = g_ref[rows, :]
    gelu = 0.5 * g * (1.0 + jnp.tanh(math.sqrt(2.0 / math.pi) * (g + 0.044715 * (g * g * g))))
    y_ref[rows, :] = (h_f + h_b) * gelu
    if write_state:
        hf_ref[s] = h_f[t_len - 1:t_len, :]
        hb_ref[s] = h_b[0:1, :]


def _rglru(z, vecs, w_gate, n_batch, t_len, state=None):
    n_seq = 1 if state is not None else 4
    rows = n_seq * t_len
    blk = lambda c0: pl.BlockSpec((rows, LANES), lambda b, j: (b, c0 // LANES + j))
    hblk = pl.BlockSpec((n_seq, 1, LANES), lambda b, j: (b, 0, j))
    in_specs = [blk(COL_LR_X), blk(COL_LR_G),
                pl.BlockSpec((11, LANES), lambda b, j: (0, j)),
                pl.BlockSpec((4, 1, LANES, LANES), lambda b, j: (0, j, 0, 0))]
    args = [z, z, vecs, w_gate]
    y_spec = pl.BlockSpec((rows, LANES), lambda b, j: (b, j))
    y_shape = jax.ShapeDtypeStruct((n_batch * t_len, BRANCH_W), F32)
    if state is not None:
        in_specs += [hblk, hblk]
        args += [s.reshape(n_batch, 1, LRU_WIDTH) for s in state]
        out_specs, out_shape = y_spec, y_shape
    else:
        h_shape = jax.ShapeDtypeStruct((n_batch, 1, LRU_WIDTH), F32)
        out_specs, out_shape = [y_spec, hblk, hblk], [y_shape, h_shape, h_shape]
    return pl.pallas_call(
        functools.partial(_lru_kernel, t_len=t_len, has_state=state is not None, write_state=state is None),
        grid=(n_batch // n_seq, LRU_WIDTH // LANES),
        in_specs=in_specs, out_specs=out_specs, out_shape=out_shape,
        compiler_params=_params("parallel", "parallel"),
        name="rglru_lat" if state is not None else "rglru_ctx",
    )(*args)


def _pair_block_diag(w):
    bw = LRU_WIDTH // LRU_BLOCKS
    w = w.reshape(LRU_BLOCKS // 2, 2, bw, bw)
    zero = jnp.zeros_like(w[:, 0])
    top = jnp.concatenate([w[:, 0], zero], axis=2)
    bot = jnp.concatenate([zero, w[:, 1]], axis=2)
    return jnp.concatenate([top, bot], axis=1)


def _merge_kernel(x_ref, ya_ref, yb_ref, yc_ref, yd_ref, wg01_ref, wg23_ref, wb_ref, wo_ref, gpre_ref, gpost_ref,
                  m_ref, o_ref):
    x = x_ref[...]
    h = (_rms(x, gpre_ref[...]) * (1.0 + m_ref[0, 1:2, :]) + m_ref[0, 0:1, :]).astype(BF16)
    acc = None
    for k, y_ref in enumerate((ya_ref, yb_ref, yc_ref, yd_ref)):
        wg_ref = wg01_ref if k < 2 else wg23_ref
        logits = jnp.dot(h, wg_ref[0, :, (k % 2) * D_MODEL:(k % 2 + 1) * D_MODEL], preferred_element_type=F32)
        term = _sigmoid(logits) * jnp.dot(y_ref[...].astype(BF16), wb_ref[0, k], preferred_element_type=F32)
        acc = term if acc is None else acc + term
    out = jnp.dot(acc.astype(BF16), wo_ref[0], preferred_element_type=F32)
    o_ref[...] = x + m_ref[0, 2:3, :] * _rms(out, gpost_ref[...])


def _merge(x, ys, mod, w_in_bf, wb_bf, wo_bf, gpre, gpost, l, mod_row):
    n_tok = x.shape[0]
    tm = 256
    yblk = pl.BlockSpec((tm, BRANCH_W), lambda i: (i, 0))
    return pl.pallas_call(
        _merge_kernel,
        grid=(n_tok // tm,),
        in_specs=[pl.BlockSpec((tm, D_MODEL), lambda i: (i, 0)), yblk, yblk, yblk, yblk,
                  pl.BlockSpec((1, D_MODEL, 2 * D_MODEL), lambda i: (l, 0, COL_GATE // (2 * D_MODEL))),
                  pl.BlockSpec((1, D_MODEL, 2 * D_MODEL), lambda i: (l, 0, COL_GATE // (2 * D_MODEL) + 1)),
                  pl.BlockSpec((1, N_BRANCH, BRANCH_W, D_MODEL), lambda i: (l, 0, 0, 0)),
                  pl.BlockSpec((1, D_MODEL, D_MODEL), lambda i: (l, 0, 0)),
                  pl.BlockSpec((1, D_MODEL), lambda i: (0, 0)),
                  pl.BlockSpec((1, D_MODEL), lambda i: (0, 0)),
                  pl.BlockSpec((1, 6, D_MODEL), lambda i: (mod_row(i, tm), 0, 0))],
        out_specs=pl.BlockSpec((tm, D_MODEL), lambda i: (i, 0)),
        out_shape=jax.ShapeDtypeStruct((n_tok, D_MODEL), F32),
        compiler_params=_params("parallel"),
        name="merge",
    )(x, *ys, w_in_bf, w_in_bf, wb_bf, wo_bf, gpre.reshape(1, D_MODEL), gpost.reshape(1, D_MODEL), mod)


def _mlp_kernel(x_ref, m_ref, gpre_ref, gpost_ref, w1_ref, w2_ref, o_ref):
    x = x_ref[...]
    h = (_rms(x, gpre_ref[...]) * (1.0 + m_ref[0, 4:5, :]) + m_ref[0, 3:4, :]).astype(BF16)
    ff_chunk = D_MODEL
    y = None
    for j in range(D_FF // ff_chunk):
        sl = slice(j * ff_chunk, (j + 1) * ff_chunk)
        a = jnp.maximum(jnp.dot(h, w1_ref[0, :, sl], preferred_element_type=F32), 0.0)
        part = jnp.dot((a * a).astype(BF16), w2_ref[0, sl, :], preferred_element_type=F32)
        y = part if y is None else y + part
    o_ref[...] = x + m_ref[0, 5:6, :] * _rms(y, gpost_ref[...])


def _mlp(x, mod, gpre, gpost, w1_bf, w2_bf, l, mod_row):
    n_tok = x.shape[0]
    tm = 256
    return pl.pallas_call(
        _mlp_kernel,
        grid=(n_tok // tm,),
        in_specs=[pl.BlockSpec((tm, D_MODEL), lambda i: (i, 0)),
                  pl.BlockSpec((1, 6, D_MODEL), lambda i: (mod_row(i, tm), 0, 0)),
                  pl.BlockSpec((1, D_MODEL), lambda i: (0, 0)),
                  pl.BlockSpec((1, D_MODEL), lambda i: (0, 0)),
                  pl.BlockSpec((1, D_MODEL, D_FF), lambda i: (l, 0, 0)),
                  pl.BlockSpec((1, D_FF, D_MODEL), lambda i: (l, 0, 0))],
        out_specs=pl.BlockSpec((tm, D_MODEL), lambda i: (i, 0)),
        out_shape=jax.ShapeDtypeStruct((n_tok, D_MODEL), F32),
        compiler_params=_params("parallel"),
        name="mlp",
    )(x, mod, gpre.reshape(1, D_MODEL), gpost.reshape(1, D_MODEL), w1_bf, w2_bf)


def _cache_heads_kernel(*refs, n_heads, dh, transposed):
    o_ref = refs[-1]
    for l, z_ref in enumerate(refs[:-1]):
        if transposed:
            for p in range(n_heads * dh // LANES):
                zt = z_ref[:, p * LANES:(p + 1) * LANES].T
                for i in range(LANES // dh):
                    o_ref[0, l, p * (LANES // dh) + i] = zt[i * dh:(i + 1) * dh, :]
        else:
            for h in range(n_heads):
                o_ref[0, l, h] = z_ref[:, h * dh:(h + 1) * dh]


def _cache_heads(zs, col, n_heads, dh):
    transposed = dh < LANES
    tail = (dh, SEQ) if transposed else (SEQ, dh)
    out = pl.pallas_call(
        functools.partial(_cache_heads_kernel, n_heads=n_heads, dh=dh, transposed=transposed),
        grid=(BATCH,),
        in_specs=[pl.BlockSpec((SEQ, 512), lambda b: (b, col // 512))] * DEPTH,
        out_specs=pl.BlockSpec((1, DEPTH, n_heads) + tail, lambda b: (b, 0, 0, 0, 0)),
        out_shape=jax.ShapeDtypeStruct((BATCH, DEPTH, n_heads) + tail, F32),
        compiler_params=_params("parallel"),
        name="cache_heads",
    )(*zs)
    return jnp.swapaxes(out, -1, -2) if transposed else out


def _ctx_mod_row(i, tm):
    return 0


def _lat_mod_row(i, tm):
    return 1 + i // (DEC_SEQ // tm)


def kernel(x_prompt, x_sample, cache_na_k, cache_na_v, cache_diff_k, cache_diff_v, state_ret_fwd, state_ret_bwd, state_lru_fwd, state_lru_bwd, c, c_ctx, ada_w, ada_b, norm_mix_pre, norm_mix_post, norm_ffn_pre, norm_ffn_post, w_in, na_rpb, diff_lq1, diff_lk1, diff_lq2, diff_lk2, diff_norm, ret_theta_fwd, ret_theta_bwd, ret_norm, lru_conv_w, lru_conv_b, lru_wa_fwd, lru_ba_fwd, lru_wx_fwd, lru_bx_fwd, lru_lam_fwd, lru_wa_bwd, lru_ba_bwd, lru_wx_bwd, lru_bx_bwd, lru_lam_bwd, w_branch, w_out, mlp_w1, mlp_w2):
    xc = x_prompt.reshape(BATCH * SEQ, D_MODEL)
    xl = x_sample.reshape(DEC_BATCH * DEC_SEQ, D_MODEL)
    cond = jnp.concatenate([c_ctx[None, :], c, jnp.zeros((COND_PAD - N_COND, D_MODEL), F32)], axis=0)
    mods = _modulation(cond.T, ada_w, ada_b).reshape(DEPTH, COND_PAD, 6, D_MODEL)
    rope_tables = _rope_tables()
    pair_tables = _na_pair_tables(_na_col_tables(na_rpb))
    w_in_bf = w_in.astype(BF16)
    wb_bf = w_branch.astype(BF16)
    wo_bf = w_out.astype(BF16)
    w1_bf = mlp_w1.astype(BF16)
    w2_bf = mlp_w2.astype(BF16)
    na_kt = jnp.swapaxes(cache_na_k, -1, -2)
    na_vt = jnp.swapaxes(cache_na_v, -1, -2)
    diff_kt = jnp.swapaxes(cache_diff_k, -1, -2)
    outs = [[] for _ in range(4)]
    zcs = []
    for l in range(DEPTH):
        mod = mods[l]
        lam_init = 0.8 - 0.6 * math.exp(-0.3 * l)
        lam_p = jnp.stack([diff_lq1[l], diff_lk1[l], diff_lq2[l], diff_lk2[l]])
        theta = jnp.stack([jnp.repeat(t, RET_DH).reshape(RET_HEADS // 2, LANES)
                           for t in (ret_theta_fwd[l], ret_theta_bwd[l])], axis=1)
        vecs = jnp.concatenate([lru_conv_w[l], lru_conv_b[l][None],
                                lru_ba_fwd[l][None], lru_bx_fwd[l][None], lru_lam_fwd[l][None],
                                lru_ba_bwd[l][None], lru_bx_bwd[l][None], lru_lam_bwd[l][None]], axis=0)
        w_gate = jnp.stack([_pair_block_diag(w) for w in
                            (lru_wa_fwd[l], lru_wx_fwd[l], lru_wa_bwd[l], lru_wx_bwd[l])])

        zc = _inproj(xc, mod, norm_mix_pre[l], w_in_bf, l, _ctx_mod_row)
        ya = _ctx_na(zc)
        yb = _ctx_diff(zc, lam_p, diff_norm[l], lam_init)
        yc, s_rf, s_rb = _retention(zc, theta, ret_norm[l], BATCH, SEQ)
        yd, h_lf, h_lb = _rglru(zc, vecs, w_gate, BATCH, SEQ)
        xc = _merge(xc, (ya, yb, yc, yd), mod, w_in_bf, wb_bf, wo_bf, norm_mix_pre[l], norm_mix_post[l], l,
                    _ctx_mod_row)
        xc = _mlp(xc, mod, norm_ffn_pre[l], norm_ffn_post[l], w1_bf, w2_bf, l, _ctx_mod_row)
        zcs.append(zc)
        outs[0].append(s_rf)
        outs[1].append(s_rb)
        outs[2].append(h_lf.reshape(BATCH, LRU_WIDTH))
        outs[3].append(h_lb.reshape(BATCH, LRU_WIDTH))

        zl = _inproj(xl, mod, norm_mix_pre[l], w_in_bf, l, _lat_mod_row)
        ya = _lat_na(zl, na_kt, na_vt, pair_tables, l)
        yb = _lat_diff(zl, diff_kt, cache_diff_v, lam_p, diff_norm[l], rope_tables, lam_init, l)
        yc = _retention(zl, theta, ret_norm[l], DEC_BATCH, DEC_SEQ,
                        state=(state_ret_fwd[:, l], state_ret_bwd[:, l]))
        yd = _rglru(zl, vecs, w_gate, DEC_BATCH, DEC_SEQ, state=(state_lru_fwd[:, l], state_lru_bwd[:, l]))
        xl = _merge(xl, (ya, yb, yc, yd), mod, w_in_bf, wb_bf, wo_bf, norm_mix_pre[l], norm_mix_post[l], l,
                    _lat_mod_row)
        xl = _mlp(xl, mod, norm_ffn_pre[l], norm_ffn_post[l], w1_bf, w2_bf, l, _lat_mod_row)

    new_na_k = _cache_heads(zcs, COL_NA_K, NA_HEADS, NA_DH)
    new_na_v = _cache_heads(zcs, COL_NA_V, NA_HEADS, NA_DH)
    new_diff_k = _cache_heads(zcs, COL_DF_K, 2 * DIFF_HEADS, DIFF_DH).reshape(
        BATCH, DEPTH, 2, DIFF_HEADS, SEQ, DIFF_DH)
    new_diff_v = _cache_heads(zcs, COL_DF_V, DIFF_HEADS, 2 * DIFF_DH)
    states = [jnp.stack(o, axis=1) for o in outs]
    return (xc.reshape(BATCH, SEQ, D_MODEL), xl.reshape(DEC_BATCH, DEC_SEQ, D_MODEL),
            new_na_k, new_na_v, new_diff_k, new_diff_v, *states)
```
